```python
import functools
import jax, jax.numpy as jnp
from jax import lax
import numpy as np

D_MODEL = 1024
BATCH = 4
SEQ = 4096
DEPTH = 1
DEC_BATCH = 128
DEC_SEQ = 8
PAST_LEN = 8192
PAGE_SIZE = 128

N_FOX_HEADS = 8
FOX_HEAD_DIM = 64
FOX_WIDTH = N_FOX_HEADS * FOX_HEAD_DIM
LRU_WIDTH = 512
N_LRU_BLOCKS = 8
LRU_BLOCK = LRU_WIDTH // N_LRU_BLOCKS
LRU_CONV_W = 4
LRU_C = 8.0
N_MEM = 256
N_MEM_HEADS = 4
MEM_HEAD_DIM = 128
MEM_WIDTH = N_MEM_HEADS * MEM_HEAD_DIM
D_FF = 3 * D_MODEL
FFN_CONV_W = 3
Q_BLOCK = 128
N_BRANCH = 3
IN_WIDTH = 3 * FOX_WIDTH + N_FOX_HEADS + 2 * LRU_WIDTH + MEM_WIDTH
EPS = 1e-6

kernel_name = "fox_rglru_memxattn_convffn_step"


def rmsnorm(x, g):
    xf = x.astype(jnp.float32)
    y = xf * lax.rsqrt(jnp.mean(xf * xf, axis=-1, keepdims=True) + EPS) * g.astype(jnp.float32)
    return y.astype(x.dtype)


def causal_dwconv(prefix, x, w, b):
    W = w.shape[0]
    T = x.shape[1]
    xp = jnp.concatenate([prefix.astype(x.dtype), x], axis=1)
    out = b
    for j in range(W):
        out = out + xp[:, j:j + T] * w[j]
    return out, xp[:, T:]


def fox_prompt_attend(q, k, v, logf):
    B, S, H, Dh = q.shape
    nb = S // Q_BLOCK
    C = jnp.cumsum(logf.astype(jnp.float32), axis=1)
    Ck = jnp.transpose(C, (0, 2, 1))[:, :, None, :]
    kpos = jnp.arange(S)
    qb = jnp.moveaxis(q.reshape(B, nb, Q_BLOCK, H, Dh), 1, 0)
    cqb = jnp.moveaxis(C.reshape(B, nb, Q_BLOCK, H), 1, 0)
    scale = Dh ** -0.5

    def one_block(args):
        qi, cqi, bi = args
        s = jnp.einsum('bqhd,bkhd->bhqk', qi, k, preferred_element_type=jnp.float32) * scale
        s = s + jnp.transpose(cqi, (0, 2, 1))[..., None] - Ck
        qpos = bi * Q_BLOCK + jnp.arange(Q_BLOCK)
        s = jnp.where(kpos[None, :] <= qpos[:, None], s, -jnp.inf)
        p = jax.nn.softmax(s, axis=-1)
        return jnp.einsum('bhqk,bkhd->bqhd', p.astype(v.dtype), v)

    o = lax.map(one_block, (qb, cqb, jnp.arange(nb)))
    return jnp.moveaxis(o, 0, 1).reshape(B, S, H, Dh)


def fox_cached_attend(k_past, v_past, logf_past, q, k, v, logf):
    P = k_past.shape[1]
    T = q.shape[1]
    Dh = q.shape[-1]
    k_all = jnp.concatenate([k_past.astype(k.dtype), k], axis=1)
    v_all = jnp.concatenate([v_past.astype(v.dtype), v], axis=1)
    C = jnp.cumsum(jnp.concatenate([logf_past.astype(jnp.float32), logf.astype(jnp.float32)], axis=1), axis=1)
    Cq = jnp.transpose(C[:, P:], (0, 2, 1))[..., None]
    Ck = jnp.transpose(C, (0, 2, 1))[:, :, None, :]
    s = jnp.einsum('bqhd,bkhd->bhqk', q, k_all, preferred_element_type=jnp.float32) * (Dh ** -0.5)
    s = s + Cq - Ck
    kpos = jnp.arange(P + T)
    qpos = P + jnp.arange(T)
    s = jnp.where(kpos[None, :] <= qpos[:, None], s, -jnp.inf)
    p = jax.nn.softmax(s, axis=-1)
    return jnp.einsum('bhqk,bkhd->bqhd', p.astype(v_all.dtype), v_all)


def mem_attend(q, mk, mv):
    s = jnp.einsum('bqhd,bmhd->bhqm', q, mk, preferred_element_type=jnp.float32) * (q.shape[-1] ** -0.5)
    p = jax.nn.softmax(s, axis=-1)
    return jnp.einsum('bhqm,bmhd->bqhd', p.astype(mv.dtype), mv)


def mem_kv(mem, g_mem, w_mem_kv):
    B, M, _ = mem.shape
    z = rmsnorm(mem, g_mem) @ w_mem_kv
    mk, mv = jnp.split(z, 2, axis=-1)
    return (mk.reshape(B, M, N_MEM_HEADS, MEM_HEAD_DIM), mv.reshape(B, M, N_MEM_HEADS, MEM_HEAD_DIM))


def rg_lru(xc, h0, w_a, b_a, w_x, b_x, lam):
    B, T, W = xc.shape
    xb = xc.reshape(B, T, N_LRU_BLOCKS, LRU_BLOCK)
    r = jax.nn.sigmoid((jnp.einsum('btnc,ncd->btnd', xb, w_a).reshape(B, T, W) + b_a).astype(jnp.float32))
    i = jax.nn.sigmoid((jnp.einsum('btnc,ncd->btnd', xb, w_x).reshape(B, T, W) + b_x).astype(jnp.float32))
    log_a = LRU_C * r * jax.nn.log_sigmoid(lam.astype(jnp.float32))
    a = jnp.exp(log_a)
    gated = jnp.sqrt(-jnp.expm1(2.0 * log_a)) * i * xc.astype(jnp.float32)

    def step(h, inp):
        a_t, g_t = inp
        h = a_t * h + g_t
        return h, h

    h_last, hs = lax.scan(step, h0.astype(jnp.float32), (jnp.swapaxes(a, 0, 1), jnp.swapaxes(gated, 0, 1)))
    return jnp.swapaxes(hs, 0, 1).astype(xc.dtype), h_last.astype(xc.dtype)


def decoder_layer(x, fox_attend, mem_k, mem_v, lru_h0, lru_buf0, ffn_buf0,
                  g_mix, w_in, b_f, w_o_fox, w_lru_conv, b_lru_conv, w_lru_a, b_lru_a,
                  w_lru_x, b_lru_x, lru_lambda, w_o_lru, w_o_mem, w_gate, b_gate, w_out,
                  g_ffn, w_ffn_gate, w_ffn_up, w_ffn_conv, b_ffn_conv, w_ffn_down):
    B, T, _ = x.shape
    u = rmsnorm(x, g_mix)
    z = u @ w_in
    sizes = (FOX_WIDTH, FOX_WIDTH, FOX_WIDTH, N_FOX_HEADS, LRU_WIDTH, LRU_WIDTH)
    bounds = [sum(sizes[:j + 1]) for j in range(len(sizes))]
    q, k, v, f_pre, x_rnn, y_rnn, q_mem = jnp.split(z, bounds, axis=-1)
    q = q.reshape(B, T, N_FOX_HEADS, FOX_HEAD_DIM)
    k = k.reshape(B, T, N_FOX_HEADS, FOX_HEAD_DIM)
    v = v.reshape(B, T, N_FOX_HEADS, FOX_HEAD_DIM)
    logf = jax.nn.log_sigmoid((f_pre + b_f).astype(jnp.float32))
    o_fox = fox_attend(q, k, v, logf).reshape(B, T, FOX_WIDTH) @ w_o_fox
    xc, lru_buf = causal_dwconv(lru_buf0, x_rnn, w_lru_conv, b_lru_conv)
    hs, h_last = rg_lru(xc, lru_h0, w_lru_a, b_lru_a, w_lru_x, b_lru_x, lru_lambda)
    o_lru = (hs * jax.nn.gelu(y_rnn)) @ w_o_lru
    o_mem = mem_attend(q_mem.reshape(B, T, N_MEM_HEADS, MEM_HEAD_DIM), mem_k, mem_v).reshape(B, T, MEM_WIDTH) @ w_o_mem
    g_a, g_b, g_c = jnp.split(jax.nn.sigmoid(u @ w_gate + b_gate), N_BRANCH, axis=-1)
    h = x + (g_a * o_fox + g_b * o_lru + g_c * o_mem) @ w_out
    v2 = rmsnorm(h, g_ffn)
    gc, ffn_buf = causal_dwconv(ffn_buf0, v2 @ w_ffn_gate, w_ffn_conv, b_ffn_conv)
    y = h + (jax.nn.gelu(gc) * (v2 @ w_ffn_up)) @ w_ffn_down
    return y, k, v, logf, h_last, lru_buf, ffn_buf


def setup_inputs(seed: int = 0) -> dict:
    key = jax.random.key(seed)
    ks = jax.random.split(key, 48)
    f32 = jnp.float32
    L = DEPTH
    n_pages = PAST_LEN // PAGE_SIZE
    n_used = DEC_BATCH * n_pages
    n_pool = n_used + max(1, n_used // 4)

    def nrm(k, shape, scale):
        return jax.random.normal(k, shape, f32) * scale

    a8 = jax.random.uniform(ks[0], (L, LRU_WIDTH), f32, 0.9, 0.999)
    a0 = a8 ** (1.0 / LRU_C)
    lru_lambda = jnp.log(a0) - jnp.log1p(-a0)
    page_table = jax.random.permutation(ks[1], n_pool)[:n_used].reshape(DEC_BATCH, n_pages).astype(jnp.int32)
    return {
        "x_prompt": nrm(ks[2], (BATCH, SEQ, D_MODEL), 1.0),
        "x_sample": nrm(ks[3], (DEC_BATCH, DEC_SEQ, D_MODEL), 1.0),
        "mem_prompt": nrm(ks[4], (BATCH, N_MEM, D_MODEL), 1.0),
        "cache_k": nrm(ks[5], (L, n_pool, PAGE_SIZE, N_FOX_HEADS, FOX_HEAD_DIM), 1.0),
        "cache_v": nrm(ks[6], (L, n_pool, PAGE_SIZE, N_FOX_HEADS, FOX_HEAD_DIM), 1.0),
        "cache_logf": jax.nn.log_sigmoid(3.0 + nrm(ks[7], (L, n_pool, PAGE_SIZE, N_FOX_HEADS), 0.5)),
        "cache_mem_k": nrm(ks[8], (L, DEC_BATCH, N_MEM, N_MEM_HEADS, MEM_HEAD_DIM), 1.0),
        "cache_mem_v": nrm(ks[9], (L, DEC_BATCH, N_MEM, N_MEM_HEADS, MEM_HEAD_DIM), 1.0),
        "state_lru_h": nrm(ks[10], (L, DEC_BATCH, LRU_WIDTH), 0.5),
        "state_lru_conv": nrm(ks[11], (L, DEC_BATCH, LRU_CONV_W - 1, LRU_WIDTH), 1.0),
        "state_ffn_conv": nrm(ks[12], (L, DEC_BATCH, FFN_CONV_W - 1, D_FF), 1.0),
        "page_table": page_table,
        "g_mix": 1.0 + nrm(ks[13], (L, D_MODEL), 0.02),
        "w_in": nrm(ks[14], (L, D_MODEL, IN_WIDTH), D_MODEL ** -0.5),
        "b_f": 2.0 + nrm(ks[15], (L, N_FOX_HEADS), 0.3),
        "w_o_fox": nrm(ks[16], (L, FOX_WIDTH, D_MODEL), FOX_WIDTH ** -0.5),
        "w_lru_conv": nrm(ks[17], (L, LRU_CONV_W, LRU_WIDTH), LRU_CONV_W ** -0.5),
        "b_lru_conv": nrm(ks[18], (L, LRU_WIDTH), 0.01),
        "w_lru_a": nrm(ks[19], (L, N_LRU_BLOCKS, LRU_BLOCK, LRU_BLOCK), LRU_BLOCK ** -0.5),
        "b_lru_a": nrm(ks[20], (L, LRU_WIDTH), 0.01),
        "w_lru_x": nrm(ks[21], (L, N_LRU_BLOCKS, LRU_BLOCK, LRU_BLOCK), LRU_BLOCK ** -0.5),
        "b_lru_x": nrm(ks[22], (L, LRU_WIDTH), 0.01),
        "lru_lambda": lru_lambda,
        "w_o_lru": nrm(ks[23], (L, LRU_WIDTH, D_MODEL), LRU_WIDTH ** -0.5),
        "g_mem": 1.0 + nrm(ks[24], (L, D_MODEL), 0.02),
        "w_mem_kv": nrm(ks[25], (L, D_MODEL, 2 * MEM_WIDTH), D_MODEL ** -0.5),
        "w_o_mem": nrm(ks[26], (L, MEM_WIDTH, D_MODEL), MEM_WIDTH ** -0.5),
        "w_gate": nrm(ks[27], (L, D_MODEL, N_BRANCH * D_MODEL), D_MODEL ** -0.5),
        "b_gate": nrm(ks[28], (L, N_BRANCH * D_MODEL), 0.01),
        "w_out": nrm(ks[29], (L, D_MODEL, D_MODEL), D_MODEL ** -0.5),
        "g_ffn": 1.0 + nrm(ks[30], (L, D_MODEL), 0.02),
        "w_ffn_gate": nrm(ks[31], (L, D_MODEL, D_FF), D_MODEL ** -0.5),
        "w_ffn_up": nrm(ks[32], (L, D_MODEL, D_FF), D_MODEL ** -0.5),
        "w_ffn_conv": nrm(ks[33], (L, FFN_CONV_W, D_FF), FFN_CONV_W ** -0.5),
        "b_ffn_conv": nrm(ks[34], (L, D_FF), 0.01),
        "w_ffn_down": nrm(ks[35], (L, D_FF, D_MODEL), D_FF ** -0.5),
        "g_final": 1.0 + nrm(ks[36], (D_MODEL,), 0.02),
    }


def reference(x_prompt, x_sample, mem_prompt, cache_k, cache_v, cache_logf, cache_mem_k, cache_mem_v,
              state_lru_h, state_lru_conv, state_ffn_conv, page_table,
              g_mix, w_in, b_f, w_o_fox, w_lru_conv, b_lru_conv, w_lru_a, b_lru_a, w_lru_x, b_lru_x,
              lru_lambda, w_o_lru, g_mem, w_mem_kv, w_o_mem, w_gate, b_gate, w_out,
              g_ffn, w_ffn_gate, w_ffn_up, w_ffn_conv, b_ffn_conv, w_ffn_down, g_final):
    B = x_prompt.shape[0]
    DB = page_table.shape[0]
    yp, ys = x_prompt, x_sample
    kp_l, vp_l, lfp_l, mkp_l, mvp_l, hp_l, lcp_l, fcp_l = [], [], [], [], [], [], [], []
    ks_l, vs_l, lfs_l, hs_l, lcs_l, fcs_l = [], [], [], [], [], []
    for l in range(DEPTH):
        lw = (g_mix[l], w_in[l], b_f[l], w_o_fox[l], w_lru_conv[l], b_lru_conv[l], w_lru_a[l], b_lru_a[l],
              w_lru_x[l], b_lru_x[l], lru_lambda[l], w_o_lru[l], w_o_mem[l], w_gate[l], b_gate[l], w_out[l],
              g_ffn[l], w_ffn_gate[l], w_ffn_up[l], w_ffn_conv[l], b_ffn_conv[l], w_ffn_down[l])
        mk_p, mv_p = mem_kv(mem_prompt, g_mem[l], w_mem_kv[l])
        h0 = jnp.zeros((B, LRU_WIDTH), x_prompt.dtype)
        lb0 = jnp.zeros((B, LRU_CONV_W - 1, LRU_WIDTH), x_prompt.dtype)
        fb0 = jnp.zeros((B, FFN_CONV_W - 1, D_FF), x_prompt.dtype)
        yp, kp, vp, lfp, hp, lcp, fcp = decoder_layer(yp, fox_prompt_attend, mk_p, mv_p, h0, lb0, fb0, *lw)
        kp_l.append(kp); vp_l.append(vp); lfp_l.append(lfp); mkp_l.append(mk_p); mvp_l.append(mv_p)
        hp_l.append(hp); lcp_l.append(lcp); fcp_l.append(fcp)
        k_past = cache_k[l][page_table].reshape(DB, -1, N_FOX_HEADS, FOX_HEAD_DIM)
        v_past = cache_v[l][page_table].reshape(DB, -1, N_FOX_HEADS, FOX_HEAD_DIM)
        lf_past = cache_logf[l][page_table].reshape(DB, -1, N_FOX_HEADS)
        attend = functools.partial(fox_cached_attend, k_past, v_past, lf_past)
        ys, kn, vn, lfn, hn, lcn, fcn = decoder_layer(ys, attend, cache_mem_k[l], cache_mem_v[l], state_lru_h[l],
                                                      state_lru_conv[l], state_ffn_conv[l], *lw)
        ks_l.append(kn); vs_l.append(vn); lfs_l.append(lfn); hs_l.append(hn); lcs_l.append(lcn); fcs_l.append(fcn)
    y_prompt = rmsnorm(yp, g_final)
    y_sample = rmsnorm(ys, g_final)
    return (y_prompt, y_sample,
            jnp.stack(kp_l), jnp.stack(vp_l), jnp.stack(lfp_l), jnp.stack(mkp_l), jnp.stack(mvp_l),
            jnp.stack(hp_l), jnp.stack(lcp_l), jnp.stack(fcp_l),
            jnp.stack(ks_l), jnp.stack(vs_l), jnp.stack(lfs_l), jnp.stack(hs_l), jnp.stack(lcs_l), jnp.stack(fcs_l))
```

```python
import functools

import jax
import jax.numpy as jnp
from jax import lax
from jax.experimental import pallas as pl
from jax.experimental.pallas import tpu as pltpu

F32 = jnp.float32
BF16 = jnp.bfloat16

EPS = 1e-6
LRU_C = 8.0
NEG_BIG = -1e30
LANES = 128
SUBLANES = 8
VMEM_LIMIT = 56 * 1024 * 1024

N_HEADS = 8
HEAD_DIM = 64
FOX_W = N_HEADS * HEAD_DIM
LRU_W = 512
MEM_HEADS = 4
MEM_DIM = 128
MEM_W = MEM_HEADS * MEM_DIM
CONV_LRU = 4
CONV_FFN = 3
PAGE = 128

TM_PROJ = 256
TM_FFN = 256
TQ = 256
FF_CHUNK = 1024
PAGES_PER_STEP = 16
MEM_BATCH = 8
SAMPLE_ROWS = 256


def _rms(x, g):
    return x * lax.rsqrt(jnp.mean(x * x, axis=-1, keepdims=True) + EPS) * g


def _log_sigmoid(x):
    return jnp.minimum(x, 0.0) - jnp.log1p(jnp.exp(-jnp.abs(x)))


def _gelu(x):
    return 0.5 * x * (1.0 + jnp.tanh(0.7978845608028654 * (x + 0.044715 * (x * x * x))))


def _dot(a, b):
    return jnp.dot(a, b, preferred_element_type=F32)


def _dot_nt(a, b):
    return lax.dot_general(a, b, (((1,), (1,)), ((), ())), preferred_element_type=F32)


def _lru_coeffs(xc, wax_ref, bax_ref, lam_ref):
    gates = _dot(xc.astype(BF16), wax_ref[...]) + bax_ref[...]
    r = jax.nn.sigmoid(gates[:, :LRU_W])
    i = jax.nn.sigmoid(gates[:, LRU_W:])
    log_a = LRU_C * r * _log_sigmoid(lam_ref[...])
    a = jnp.exp(log_a)
    th = jnp.tanh(log_a)
    gated = jnp.sqrt(-2.0 * th / (1.0 - th)) * i * xc
    return a, gated


def _mem_attend_head(q, mk, mv):
    s = _dot_nt(q, mk)
    m = jnp.max(s, axis=-1, keepdims=True)
    p = jnp.exp(s - m)
    l = jnp.sum(p, axis=-1, keepdims=True)
    return _dot(p.astype(BF16), mv) / l


def _lane_cumsum(x, width):
    lane = lax.broadcasted_iota(jnp.int32, x.shape, x.ndim - 1)
    d = 1
    while d < width:
        x = x + jnp.where(lane >= d, pltpu.roll(x, d, axis=x.ndim - 1), 0.0)
        d *= 2
    return x


def _park(scr, x):
    for c in range(scr.shape[0]):
        scr[c] = x[:, c * LANES:(c + 1) * LANES]


def _unpark(scr):
    return jnp.concatenate([scr[c] for c in range(scr.shape[0])], axis=1)


def _token_rows(scr, t, nseq, ntok):
    return jnp.concatenate([scr[c, pl.ds(t, nseq, stride=ntok), :] for c in range(scr.shape[0])],
                           axis=1)


def _set_token_rows(scr, t, nseq, ntok, x):
    for c in range(scr.shape[0]):
        scr[c, pl.ds(t, nseq, stride=ntok), :] = x[:, c * LANES:(c + 1) * LANES]


def _const_spec(shape):
    nd = len(shape)
    return pl.BlockSpec(shape, lambda *_: (0,) * nd, pipeline_mode=pl.Buffered(1))


def _mem_kv_kernel(mem_ref, g_ref, w_ref, mk_ref, mv_ref):
    u = _rms(mem_ref[0], g_ref[...]).astype(BF16)
    z = _dot(u, w_ref[...])
    mk_ref[0] = z[:, :MEM_W]
    mv_ref[0] = z[:, MEM_W:]


def _mem_kv(mem, g_mem, w_mem_kv):
    B, M, D = mem.shape
    return pl.pallas_call(
        _mem_kv_kernel,
        grid=(B,),
        in_specs=[pl.BlockSpec((1, M, D), lambda b: (b, 0, 0)),
                  _const_spec((1, D)), _const_spec((D, 2 * MEM_W))],
        out_specs=[pl.BlockSpec((1, M, MEM_W), lambda b: (b, 0, 0))] * 2,
        out_shape=[jax.ShapeDtypeStruct((B, M, MEM_W), F32)] * 2,
        compiler_params=pltpu.CompilerParams(dimension_semantics=("arbitrary",),
                                             vmem_limit_bytes=VMEM_LIMIT),
        name="mem_kv",
    )(mem, g_mem, w_mem_kv)


def _proj_prompt_kernel(x_ref, gmix_ref, wqkv_ref, wf_ref, bf_ref, wrnn_ref, wqm_ref,
                        wconv_ref, bconv_ref, wax_ref, bax_ref, lam_ref, mk_ref, mv_ref,
                        k_ref, v_ref, logf_ref, ck_ref, qb_ref, kb_ref, vb_ref,
                        olru_ref, omem_ref, xtail_ref, htail_ref,
                        xext_scr, a_scr, g_scr, h_scr, c_scr):
    tm = x_ref.shape[1]
    pad = a_scr.shape[0] - tm
    t = pl.program_id(1)

    @pl.when(t == 0)
    def _():
        xext_scr[0:SUBLANES, :] = jnp.zeros((SUBLANES, LRU_W), F32)
        h_scr[...] = jnp.zeros_like(h_scr)
        c_scr[...] = jnp.zeros_like(c_scr)
        a_scr[0:pad, :] = jnp.ones((pad, LRU_W), F32)
        g_scr[0:pad, :] = jnp.zeros((pad, LRU_W), F32)

    u = _rms(x_ref[0], gmix_ref[...]).astype(BF16)

    zqkv = _dot(u, wqkv_ref[...])
    k_ref[0] = zqkv[:, FOX_W:2 * FOX_W]
    v_ref[0] = zqkv[:, 2 * FOX_W:]
    qb_ref[0] = zqkv[:, :FOX_W].astype(BF16)
    kb_ref[0] = zqkv[:, FOX_W:2 * FOX_W].astype(BF16)
    vb_ref[0] = zqkv[:, 2 * FOX_W:].astype(BF16)

    logf = _log_sigmoid(_dot(u, wf_ref[...]) + bf_ref[...])
    logf_ref[0] = logf[:, :N_HEADS]
    c = _lane_cumsum(logf.T[:N_HEADS, :], tm) + c_scr[:, 0:1]
    ck_ref[0] = c
    c_scr[...] = jnp.broadcast_to(c[:, tm - 1:tm], c_scr.shape)

    zr = _dot(u, wrnn_ref[...])
    xr = zr[:, :LRU_W]
    xext_scr[SUBLANES:SUBLANES + tm, :] = xr
    xtail_ref[0] = xr[tm - SUBLANES:, :]
    xc = bconv_ref[...] + wconv_ref[CONV_LRU - 1:CONV_LRU, :] * xr
    for j in range(CONV_LRU - 1):
        off = SUBLANES - (CONV_LRU - 1) + j
        xc = xc + wconv_ref[j:j + 1, :] * xext_scr[off:off + tm, :]
    xext_scr[0:SUBLANES, :] = xr[tm - SUBLANES:, :]

    a, gated = _lru_coeffs(xc, wax_ref, bax_ref, lam_ref)
    d = 1
    while d < tm:
        a_scr[pad:pad + tm, :] = a
        g_scr[pad:pad + tm, :] = gated
        gated = gated + a * g_scr[pad - d:pad - d + tm, :]
        a = a * a_scr[pad - d:pad - d + tm, :]
        d *= 2
    hs = gated + a * h_scr[0:1, :]
    h_scr[...] = jnp.broadcast_to(hs[tm - 1:tm, :], h_scr.shape)
    htail_ref[0] = hs[tm - SUBLANES:, :]
    olru_ref[0] = (hs * _gelu(zr[:, LRU_W:])).astype(BF16)

    zq = _dot(u, wqm_ref[...]) * (MEM_DIM ** -0.5)
    for h in range(MEM_HEADS):
        sl = slice(h * MEM_DIM, (h + 1) * MEM_DIM)
        o = _mem_attend_head(zq[:, sl].astype(BF16), mk_ref[0, :, sl].astype(BF16),
                             mv_ref[0, :, sl].astype(BF16))
        omem_ref[0, :, sl] = o.astype(BF16)


def _proj_prompt(x, mk, mv, w):
    B, S, D = x.shape
    M = mk.shape[1]
    tm = TM_PROJ
    nt = S // tm
    row = lambda width: pl.BlockSpec((1, tm, width), lambda b, t: (b, t, 0))
    per_b = lambda rows, width: pl.BlockSpec((1, rows, width), lambda b, t: (b, 0, 0))
    consts = [w["g_mix"], w["w_qkv"], w["w_f"], w["b_f"], w["w_rnn"], w["w_qm"],
              w["w_lru_conv"], w["b_lru_conv"], w["w_ax"], w["b_ax"], w["lam"]]
    out_shape = [
        jax.ShapeDtypeStruct((B, S, FOX_W), F32),
        jax.ShapeDtypeStruct((B, S, FOX_W), F32),
        jax.ShapeDtypeStruct((B, S, N_HEADS), F32),
        jax.ShapeDtypeStruct((B, N_HEADS, S), F32),
        jax.ShapeDtypeStruct((B, S, FOX_W), BF16),
        jax.ShapeDtypeStruct((B, S, FOX_W), BF16),
        jax.ShapeDtypeStruct((B, S, FOX_W), BF16),
        jax.ShapeDtypeStruct((B, S, LRU_W), BF16),
        jax.ShapeDtypeStruct((B, S, MEM_W), BF16),
        jax.ShapeDtypeStruct((B, SUBLANES, LRU_W), F32),
        jax.ShapeDtypeStruct((B, SUBLANES, LRU_W), F32),
    ]
    out_specs = [row(FOX_W), row(FOX_W), row(N_HEADS),
                 pl.BlockSpec((1, N_HEADS, tm), lambda b, t: (b, 0, t)),
                 row(FOX_W), row(FOX_W), row(FOX_W), row(LRU_W), row(MEM_W),
                 per_b(SUBLANES, LRU_W), per_b(SUBLANES, LRU_W)]
    pad = tm // 2
    return pl.pallas_call(
        _proj_prompt_kernel,
        grid=(B, nt),
        in_specs=[row(D)] + [_const_spec(c.shape) for c in consts] + [per_b(M, MEM_W)] * 2,
        out_specs=out_specs,
        out_shape=out_shape,
        scratch_shapes=[pltpu.VMEM((tm + SUBLANES, LRU_W), F32),
                        pltpu.VMEM((tm + pad, LRU_W), F32),
                        pltpu.VMEM((tm + pad, LRU_W), F32),
                        pltpu.VMEM((SUBLANES, LRU_W), F32),
                        pltpu.VMEM((N_HEADS, LANES), F32)],
        compiler_params=pltpu.CompilerParams(dimension_semantics=("arbitrary", "arbitrary"),
                                             vmem_limit_bytes=VMEM_LIMIT),
        name="proj_prompt",
    )(x, *consts, mk, mv)


def _fox_prompt_kernel(q_ref, k_ref, v_ref, ck_ref, o_ref):
    tq = q_ref.shape[1]
    i = pl.program_id(2)
    q2 = q_ref[0]
    lane = lax.broadcasted_iota(jnp.int32, q2.shape, 1)
    first = lane < HEAD_DIM
    rowi = lax.broadcasted_iota(jnp.int32, (tq, tq), 0)
    coli = lax.broadcasted_iota(jnp.int32, (tq, tq), 1)
    causal = coli <= rowi
    outs = []
    for hh in range(2):
        qh = jnp.where(first if hh == 0 else jnp.logical_not(first), q2, jnp.zeros_like(q2))

        def block(j, carry, masked):
            m, l, acc = carry
            start = pl.multiple_of(j * tq, tq)
            kb = k_ref[0, pl.ds(start, tq), :]
            vb = v_ref[0, pl.ds(start, tq), :]
            s = _dot_nt(qh, kb) - ck_ref[0, 0, hh:hh + 1, pl.ds(start, tq)]
            if masked:
                s = jnp.where(causal, s, NEG_BIG)
            m_new = jnp.maximum(m, jnp.max(s, axis=-1, keepdims=True))
            alpha = jnp.exp(m - m_new)
            p = jnp.exp(s - m_new)
            l = alpha * l + jnp.sum(p, axis=-1, keepdims=True)
            acc = alpha * acc + _dot(p.astype(BF16), vb)
            return m_new, l, acc

        init = (jnp.full((tq, 1), NEG_BIG, F32), jnp.zeros((tq, 1), F32),
                jnp.zeros((tq, LANES), F32))
        carry = lax.fori_loop(0, i, functools.partial(block, masked=False), init)
        m, l, acc = block(i, carry, True)
        outs.append(acc / l)
    o_ref[0] = jnp.where(first, outs[0], outs[1]).astype(o_ref.dtype)


def _fox_prompt(qb, kb, vb, ck):
    B, S, W = qb.shape
    npair = W // LANES
    ck4 = ck.reshape(B, npair, 2, S)
    return pl.pallas_call(
        _fox_prompt_kernel,
        grid=(B, npair, S // TQ),
        in_specs=[pl.BlockSpec((1, TQ, LANES), lambda b, h, i: (b, i, h)),
                  pl.BlockSpec((1, S, LANES), lambda b, h, i: (b, 0, h)),
                  pl.BlockSpec((1, S, LANES), lambda b, h, i: (b, 0, h)),
                  pl.BlockSpec((1, 1, 2, S), lambda b, h, i: (b, h, 0, 0))],
        out_specs=pl.BlockSpec((1, TQ, LANES), lambda b, h, i: (b, i, h)),
        out_shape=jax.ShapeDtypeStruct((B, S, W), BF16),
        compiler_params=pltpu.CompilerParams(
            dimension_semantics=("arbitrary", "arbitrary", "arbitrary"),
            vmem_limit_bytes=VMEM_LIMIT),
        name="fox_prompt",
    )(qb, kb, vb, ck4)


def _merge(x, branches, gmix_ref, wgate_ref, bgate_ref, wout_ref):
    D = x.shape[1]
    u = _rms(x, gmix_ref[...]).astype(BF16)
    merged = None
    for j, (o, wo_ref) in enumerate(branches):
        sl = slice(j * D, (j + 1) * D)
        gate = jax.nn.sigmoid(_dot(u, wgate_ref[:, sl]) + bgate_ref[:, sl])
        term = gate * _dot(o, wo_ref[...])
        merged = term if merged is None else merged + term
    return x + _dot(merged.astype(BF16), wout_ref[...])


def _merge_ffn_prompt_kernel(x_ref, ofox_ref, olru_ref, omem_ref, gmix_ref, wgate_ref, bgate_ref,
                             wofox_ref, wolru_ref, womem_ref, wout_ref, gffn_ref, wfg_ref, wfu_ref,
                             wfc_ref, bfc_ref, wfd_ref, gfin_ref,
                             y_ref, gtail_ref, gext_scr):
    tm = x_ref.shape[1]
    dff = wfg_ref.shape[1]
    t = pl.program_id(1)

    @pl.when(t == 0)
    def _():
        gext_scr[0:SUBLANES, :] = jnp.zeros((SUBLANES, dff), F32)

    h = _merge(x_ref[0], [(ofox_ref[0], wofox_ref), (olru_ref[0], wolru_ref),
                          (omem_ref[0], womem_ref)], gmix_ref, wgate_ref, bgate_ref, wout_ref)
    v2 = _rms(h, gffn_ref[...]).astype(BF16)
    y = h
    for c in range(dff // FF_CHUNK):
        sl = slice(c * FF_CHUNK, (c + 1) * FF_CHUNK)
        g = _dot(v2, wfg_ref[:, sl])
        gext_scr[SUBLANES:SUBLANES + tm, sl] = g
        gc = bfc_ref[:, sl] + wfc_ref[CONV_FFN - 1:CONV_FFN, sl] * g
        for j in range(CONV_FFN - 1):
            off = SUBLANES - (CONV_FFN - 1) + j
            gc = gc + wfc_ref[j:j + 1, sl] * gext_scr[off:off + tm, sl]
        act = (_gelu(gc) * _dot(v2, wfu_ref[:, sl])).astype(BF16)
        y = y + _dot(act, wfd_ref[sl, :])
    tail = gext_scr[tm:tm + SUBLANES, :]
    gtail_ref[0] = tail
    gext_scr[0:SUBLANES, :] = tail
    y_ref[0] = _rms(y, gfin_ref[...])


def _merge_ffn_consts(w):
    return [w["g_mix"], w["w_gate"], w["b_gate"], w["w_o_fox"], w["w_o_lru"], w["w_o_mem"],
            w["w_out"], w["g_ffn"], w["w_ffn_gate"], w["w_ffn_up"], w["w_ffn_conv"],
            w["b_ffn_conv"], w["w_ffn_down"], w["g_final"]]


def _merge_ffn_prompt(x, ofox, olru, omem, w):
    B, S, D = x.shape
    dff = w["w_ffn_gate"].shape[1]
    tm = TM_FFN
    row = lambda width: pl.BlockSpec((1, tm, width), lambda b, t: (b, t, 0))
    consts = _merge_ffn_consts(w)
    return pl.pallas_call(
        _merge_ffn_prompt_kernel,
        grid=(B, S // tm),
        in_specs=[row(D), row(FOX_W), row(LRU_W), row(MEM_W)]
                 + [_const_spec(c.shape) for c in consts],
        out_specs=[row(D), pl.BlockSpec((1, SUBLANES, dff), lambda b, t: (b, 0, 0))],
        out_shape=[jax.ShapeDtypeStruct((B, S, D), F32),
                   jax.ShapeDtypeStruct((B, SUBLANES, dff), F32)],
        scratch_shapes=[pltpu.VMEM((tm + SUBLANES, dff), F32)],
        compiler_params=pltpu.CompilerParams(dimension_semantics=("arbitrary", "arbitrary"),
                                             vmem_limit_bytes=VMEM_LIMIT),
        name="merge_ffn_prompt",
    )(x, ofox, olru, omem, *consts)


def _proj_sample_kernel(x_ref, gmix_ref, wqkv_ref, wf_ref, bf_ref, wrnn_ref, wqm_ref,
                        wconv_ref, bconv_ref, wax_ref, bax_ref, lam_ref, cstate_ref, h0_ref,
                        q_ref, k_ref, v_ref, logf_ref, qm_ref, olru_ref, xtail_ref, hlast_ref,
                        x_scr, h_scr):
    nb = h0_ref.shape[0]
    nt = x_ref.shape[0] // nb
    u = _rms(x_ref[...], gmix_ref[...]).astype(BF16)
    zqkv = _dot(u, wqkv_ref[...])
    q_ref[...] = zqkv[:, :FOX_W]
    k_ref[...] = zqkv[:, FOX_W:2 * FOX_W]
    v_ref[...] = zqkv[:, 2 * FOX_W:]
    logf = _log_sigmoid(_dot(u, wf_ref[...]) + bf_ref[...])
    logf_ref[...] = logf[:, :N_HEADS]
    qm_ref[...] = _dot(u, wqm_ref[...]) * (MEM_DIM ** -0.5)

    zr = _dot(u, wrnn_ref[...])
    _park(x_scr, zr[:, :LRU_W])
    xs = [cstate_ref[j] for j in range(CONV_LRU - 1)]
    xs += [_token_rows(x_scr, t, nb, nt) for t in range(nt)]
    h = h0_ref[...]
    for t in range(nt):
        xc = bconv_ref[...]
        for j in range(CONV_LRU):
            xc = xc + wconv_ref[j:j + 1, :] * xs[t + j]
        a, gated = _lru_coeffs(xc, wax_ref, bax_ref, lam_ref)
        h = a * h + gated
        _set_token_rows(h_scr, t, nb, nt, h)
    hlast_ref[...] = h
    for j in range(CONV_LRU - 1):
        xtail_ref[j] = xs[nt + j]
    olru_ref[...] = (_unpark(h_scr) * _gelu(zr[:, LRU_W:])).astype(BF16)


def _proj_sample(x2, cstate_tm, h0, w):
    R, D = x2.shape
    nb = h0.shape[0]
    nt = R // nb
    rb = SAMPLE_ROWS
    sb = rb // nt
    consts = [w["g_mix"], w["w_qkv"], w["w_f"], w["b_f"], w["w_rnn"], w["w_qm"],
              w["w_lru_conv"], w["b_lru_conv"], w["w_ax"], w["b_ax"], w["lam"]]
    row = lambda width: pl.BlockSpec((rb, width), lambda i: (i, 0))
    state = pl.BlockSpec((CONV_LRU - 1, sb, LRU_W), lambda i: (0, i, 0))
    seq = pl.BlockSpec((sb, LRU_W), lambda i: (i, 0))
    out_shape = [
        jax.ShapeDtypeStruct((R, FOX_W), F32),
        jax.ShapeDtypeStruct((R, FOX_W), F32),
        jax.ShapeDtypeStruct((R, FOX_W), F32),
        jax.ShapeDtypeStruct((R, N_HEADS), F32),
        jax.ShapeDtypeStruct((R, MEM_W), F32),
        jax.ShapeDtypeStruct((R, LRU_W), BF16),
        jax.ShapeDtypeStruct((CONV_LRU - 1, nb, LRU_W), F32),
        jax.ShapeDtypeStruct((nb, LRU_W), F32),
    ]
    return pl.pallas_call(
        _proj_sample_kernel,
        grid=(R // rb,),
        in_specs=[row(D)] + [_const_spec(c.shape) for c in consts] + [state, seq],
        out_specs=[row(FOX_W), row(FOX_W), row(FOX_W), row(N_HEADS), row(MEM_W), row(LRU_W),
                   state, seq],
        out_shape=out_shape,
        scratch_shapes=[pltpu.VMEM((LRU_W // LANES, rb, LANES), F32)] * 2,
        compiler_params=pltpu.CompilerParams(dimension_semantics=("arbitrary",),
                                             vmem_limit_bytes=VMEM_LIMIT),
        name="proj_sample",
    )(x2, *consts, cstate_tm, h0)


def _mem_sample_kernel(q_ref, mk_ref, mv_ref, o_ref):
    nb = mk_ref.shape[0]
    nt = q_ref.shape[0] // nb
    for b in range(nb):
        rows = slice(b * nt, (b + 1) * nt)
        for h in range(MEM_HEADS):
            sl = slice(h * MEM_DIM, (h + 1) * MEM_DIM)
            o_ref[rows, sl] = _mem_attend_head(q_ref[rows, sl].astype(BF16),
                                               mk_ref[b, :, sl].astype(BF16),
                                               mv_ref[b, :, sl].astype(BF16))


def _mem_sample(qm, mk, mv):
    R, W = qm.shape
    DB, M, _ = mk.shape
    nt = R // DB
    nb = MEM_BATCH
    return pl.pallas_call(
        _mem_sample_kernel,
        grid=(DB // nb,),
        in_specs=[pl.BlockSpec((nb * nt, W), lambda i: (i, 0)),
                  pl.BlockSpec((nb, M, W), lambda i: (i, 0, 0)),
                  pl.BlockSpec((nb, M, W), lambda i: (i, 0, 0))],
        out_specs=pl.BlockSpec((nb * nt, W), lambda i: (i, 0)),
        out_shape=jax.ShapeDtypeStruct((R, W), F32),
        compiler_params=pltpu.CompilerParams(dimension_semantics=("arbitrary",),
                                             vmem_limit_bytes=VMEM_LIMIT),
        name="mem_sample",
    )(qm, mk, mv)


def _logf_cumsum_kernel(pt_ref, lfn_ref, *refs):
    npg = PAGES_PER_STEP
    pages = refs[:npg]
    cb_ref, cn_ref, pack_scr, carry_scr = refs[npg:]
    c = pl.program_id(1)

    @pl.when(c == 0)
    def _():
        carry_scr[...] = jnp.zeros_like(carry_scr)

    for q in range(npg):
        pack_scr[:, q * N_HEADS:(q + 1) * N_HEADS] = pages[q][0]
    z = _lane_cumsum(pack_scr[...].T, PAGE)
    carry = carry_scr[...]
    for q in range(npg):
        tile = z[q * N_HEADS:(q + 1) * N_HEADS, :] + carry
        cb_ref[0, :, q * PAGE:(q + 1) * PAGE] = tile
        carry = jnp.broadcast_to(tile[:, PAGE - 1:PAGE], carry.shape)
    carry_scr[...] = carry

    @pl.when(c == pl.num_programs(1) - 1)
    def _():
        pack_scr[...] = jnp.zeros_like(pack_scr)
        pack_scr[0:lfn_ref.shape[1], 0:N_HEADS] = lfn_ref[0]
        zn = _lane_cumsum(pack_scr[...].T[:N_HEADS, :], PAGE)
        cn_ref[0] = zn + carry_scr[...]


def _logf_cumsum(page_table, cache_logf, logf_new):
    DB, npages = page_table.shape
    nt = logf_new.shape[1]
    npg = PAGES_PER_STEP

    def page_spec(q):
        return pl.BlockSpec((1, PAGE, N_HEADS), lambda b, c, pt: (pt[b, c * npg + q], 0, 0))

    grid_spec = pltpu.PrefetchScalarGridSpec(
        num_scalar_prefetch=1,
        grid=(DB, npages // npg),
        in_specs=[pl.BlockSpec((1, nt, N_HEADS), lambda b, c, pt: (b, 0, 0))]
                 + [page_spec(q) for q in range(npg)],
        out_specs=[pl.BlockSpec((1, N_HEADS, npg * PAGE), lambda b, c, pt: (b, 0, c)),
                   pl.BlockSpec((1, N_HEADS, PAGE), lambda b, c, pt: (b, 0, 0))],
        scratch_shapes=[pltpu.VMEM((PAGE, LANES), F32), pltpu.VMEM((N_HEADS, LANES), F32)],
    )
    return pl.pallas_call(
        _logf_cumsum_kernel,
        grid_spec=grid_spec,
        out_shape=[jax.ShapeDtypeStruct((DB, N_HEADS, npages * PAGE), F32),
                   jax.ShapeDtypeStruct((DB, N_HEADS, PAGE), F32)],
        compiler_params=pltpu.CompilerParams(dimension_semantics=("arbitrary", "arbitrary"),
                                             vmem_limit_bytes=VMEM_LIMIT),
        name="logf_cumsum",
    )(page_table, logf_new, *([cache_logf] * npg))


def _fox_sample_kernel(pt_ref, q_ref, kn_ref, vn_ref, cb_ref, cn_ref, *refs):
    npg = PAGES_PER_STEP
    k_pages = refs[:npg]
    v_pages = refs[npg:2 * npg]
    o_ref, kb_scr, vb_scr, qbd_scr, m_scr, l_scr, acc_scr = refs[2 * npg:]
    nt = q_ref.shape[0]
    nrow = N_HEADS * nt
    c = pl.program_id(1)
    lane = lax.broadcasted_iota(jnp.int32, (nt, FOX_W), 1)

    @pl.when(c == 0)
    def _():
        q = q_ref[...]
        for h in range(N_HEADS):
            mine = (lane >= h * HEAD_DIM) & (lane < (h + 1) * HEAD_DIM)
            qbd_scr[h * nt:(h + 1) * nt, :] = jnp.where(mine, q, 0.0).astype(BF16)
        m_scr[...] = jnp.full_like(m_scr, NEG_BIG)
        l_scr[...] = jnp.zeros_like(l_scr)
        acc_scr[...] = jnp.zeros_like(acc_scr)

    def update(s, vb):
        m = m_scr[...]
        m_new = jnp.maximum(m, jnp.max(s, axis=-1, keepdims=True))
        alpha = jnp.exp(m - m_new)
        p = jnp.exp(s - m_new)
        l_scr[...] = alpha * l_scr[...] + jnp.sum(p, axis=-1, keepdims=True)
        acc_scr[...] = alpha * acc_scr[...] + _dot(p.astype(BF16), vb)
        m_scr[...] = m_new

    def head_rows(bias_ref):
        nk = bias_ref.shape[2]
        return jnp.concatenate(
            [jnp.broadcast_to(bias_ref[0, h:h + 1, :], (nt, nk)) for h in range(N_HEADS)], axis=0)

    for p in range(npg):
        kb_scr[p * PAGE:(p + 1) * PAGE, :] = k_pages[p][0].astype(BF16)
        vb_scr[p * PAGE:(p + 1) * PAGE, :] = v_pages[p][0].astype(BF16)
    update(_dot_nt(qbd_scr[...], kb_scr[...]) - head_rows(cb_ref), vb_scr[...])

    @pl.when(c == pl.num_programs(1) - 1)
    def _():
        zeros = jnp.zeros((PAGE - nt, FOX_W), F32)
        kn = jnp.concatenate([kn_ref[...], zeros], axis=0).astype(BF16)
        vn = jnp.concatenate([vn_ref[...], zeros], axis=0).astype(BF16)
        s = _dot_nt(qbd_scr[...], kn) - head_rows(cn_ref)
        tok = lax.broadcasted_iota(jnp.int32, (nrow, PAGE), 0) % nt
        key = lax.broadcasted_iota(jnp.int32, (nrow, PAGE), 1)
        update(jnp.where(key <= tok, s, NEG_BIG), vn)
        acc = acc_scr[...] / l_scr[...]
        out = acc[0:nt, :]
        for h in range(1, N_HEADS):
            out = jnp.where(lane >= h * HEAD_DIM, acc[h * nt:(h + 1) * nt, :], out)
        o_ref[...] = out


def _fox_sample(page_table, q, k_new, v_new, cb, cn, cache_k, cache_v):
    DB, npages = page_table.shape
    R, W = q.shape
    nt = R // DB
    npg = PAGES_PER_STEP
    nk = npg * PAGE

    def page_spec(j):
        return pl.BlockSpec((1, PAGE, W), lambda b, c, pt: (pt[b, c * npg + j], 0, 0))

    tok = pl.BlockSpec((nt, W), lambda b, c, pt: (b, 0))
    grid_spec = pltpu.PrefetchScalarGridSpec(
        num_scalar_prefetch=1,
        grid=(DB, npages // npg),
        in_specs=[tok, tok, tok,
                  pl.BlockSpec((1, N_HEADS, nk), lambda b, c, pt: (b, 0, c)),
                  pl.BlockSpec((1, N_HEADS, PAGE), lambda b, c, pt: (b, 0, 0))]
                 + [page_spec(j) for j in range(npg)] * 2,
        out_specs=tok,
        scratch_shapes=[pltpu.VMEM((nk, W), BF16), pltpu.VMEM((nk, W), BF16),
                        pltpu.VMEM((N_HEADS * nt, W), BF16),
                        pltpu.VMEM((N_HEADS * nt, 1), F32), pltpu.VMEM((N_HEADS * nt, 1), F32),
                        pltpu.VMEM((N_HEADS * nt, W), F32)],
    )
    return pl.pallas_call(
        _fox_sample_kernel,
        grid_spec=grid_spec,
        out_shape=jax.ShapeDtypeStruct((R, W), F32),
        compiler_params=pltpu.CompilerParams(dimension_semantics=("arbitrary", "arbitrary"),
                                             vmem_limit_bytes=VMEM_LIMIT),
        name="fox_sample",
    )(page_table, q, k_new, v_new, cb, cn, *([cache_k] * npg), *([cache_v] * npg))


def _merge_ffn_sample_kernel(x_ref, ofox_ref, olru_ref, omem_ref, fstate_ref, gmix_ref, wgate_ref,
                             bgate_ref, wofox_ref, wolru_ref, womem_ref, wout_ref, gffn_ref,
                             wfg_ref, wfu_ref, wfc_ref, bfc_ref, wfd_ref, gfin_ref,
                             y_ref, gtail_ref, g_scr, gc_scr):
    nb = fstate_ref.shape[1]
    nt = x_ref.shape[0] // nb
    dff = wfg_ref.shape[1]
    h = _merge(x_ref[...], [(ofox_ref[...].astype(BF16), wofox_ref), (olru_ref[...], wolru_ref),
                            (omem_ref[...].astype(BF16), womem_ref)],
               gmix_ref, wgate_ref, bgate_ref, wout_ref)
    v2 = _rms(h, gffn_ref[...]).astype(BF16)
    y = h
    for c in range(dff // FF_CHUNK):
        sl = slice(c * FF_CHUNK, (c + 1) * FF_CHUNK)
        _park(g_scr, _dot(v2, wfg_ref[:, sl]))
        gs = [fstate_ref[j, :, sl] for j in range(CONV_FFN - 1)]
        gs += [_token_rows(g_scr, t, nb, nt) for t in range(nt)]
        for t in range(nt):
            gc = bfc_ref[:, sl]
            for j in range(CONV_FFN):
                gc = gc + wfc_ref[j:j + 1, sl] * gs[t + j]
            _set_token_rows(gc_scr, t, nb, nt, gc)
        for j in range(CONV_FFN - 1):
            gtail_ref[j, :, sl] = gs[nt + j]
        act = (_gelu(_unpark(gc_scr)) * _dot(v2, wfu_ref[:, sl])).astype(BF16)
        y = y + _dot(act, wfd_ref[sl, :])
    y_ref[...] = _rms(y, gfin_ref[...])


def _merge_ffn_sample(x2, ofox, olru, omem, fstate_tm, w):
    R, D = x2.shape
    nb = fstate_tm.shape[1]
    nt = R // nb
    dff = w["w_ffn_gate"].shape[1]
    rb = SAMPLE_ROWS
    sb = rb // nt
    consts = _merge_ffn_consts(w)
    row = lambda width: pl.BlockSpec((rb, width), lambda i: (i, 0))
    state = pl.BlockSpec((CONV_FFN - 1, sb, dff), lambda i: (0, i, 0))
    return pl.pallas_call(
        _merge_ffn_sample_kernel,
        grid=(R // rb,),
        in_specs=[row(D), row(FOX_W), row(LRU_W), row(MEM_W), state]
                 + [_const_spec(c.shape) for c in consts],
        out_specs=[row(D), state],
        out_shape=[jax.ShapeDtypeStruct((R, D), F32),
                   jax.ShapeDtypeStruct((CONV_FFN - 1, nb, dff), F32)],
        scratch_shapes=[pltpu.VMEM((FF_CHUNK // LANES, rb, LANES), F32)] * 2,
        compiler_params=pltpu.CompilerParams(dimension_semantics=("arbitrary",),
                                             vmem_limit_bytes=VMEM_LIMIT),
        name="merge_ffn_sample",
    )(x2, ofox, olru, omem, fstate_tm, *consts)


def _block_diag(wb):
    nb, bs, _ = wb.shape
    eye = jnp.eye(nb, dtype=wb.dtype)
    return (eye[:, None, :, None] * wb[:, :, None, :]).reshape(nb * bs, nb * bs)


def _prep_weights(l, g_mix, w_in, b_f, w_o_fox, w_lru_conv, b_lru_conv, w_lru_a, b_lru_a, w_lru_x,
                  b_lru_x, lru_lambda, w_o_lru, w_o_mem, w_gate, b_gate, w_out, g_ffn, w_ffn_gate,
                  w_ffn_up, w_ffn_conv, b_ffn_conv, w_ffn_down, g_final):
    wi = w_in[l]
    o_f = 3 * FOX_W
    o_r = o_f + N_HEADS
    o_m = o_r + 2 * LRU_W
    row = lambda v: v.reshape(1, -1)
    w_qkv = jnp.concatenate([wi[:, :FOX_W] * (HEAD_DIM ** -0.5), wi[:, FOX_W:o_f]], axis=1)
    return {
        "g_mix": row(g_mix[l]),
        "w_qkv": w_qkv.astype(BF16),
        "w_f": jnp.pad(wi[:, o_f:o_r], ((0, 0), (0, LANES - N_HEADS))).astype(BF16),
        "b_f": jnp.pad(row(b_f[l]), ((0, 0), (0, LANES - N_HEADS))),
        "w_rnn": wi[:, o_r:o_m].astype(BF16),
        "w_qm": wi[:, o_m:].astype(BF16),
        "w_lru_conv": w_lru_conv[l],
        "b_lru_conv": row(b_lru_conv[l]),
        "w_ax": jnp.concatenate([_block_diag(w_lru_a[l]), _block_diag(w_lru_x[l])],
                                axis=1).astype(BF16),
        "b_ax": jnp.concatenate([row(b_lru_a[l]), row(b_lru_x[l])], axis=1),
        "lam": row(lru_lambda[l]),
        "w_gate": w_gate[l].astype(BF16),
        "b_gate": row(b_gate[l]),
        "w_o_fox": w_o_fox[l].astype(BF16),
        "w_o_lru": w_o_lru[l].astype(BF16),
        "w_o_mem": w_o_mem[l].astype(BF16),
        "w_out": w_out[l].astype(BF16),
        "g_ffn": row(g_ffn[l]),
        "w_ffn_gate": w_ffn_gate[l].astype(BF16),
        "w_ffn_up": w_ffn_up[l].astype(BF16),
        "w_ffn_conv": w_ffn_conv[l],
        "b_ffn_conv": row(b_ffn_conv[l]),
        "w_ffn_down": w_ffn_down[l].astype(BF16),
        "g_final": row(g_final),
    }


def kernel(x_prompt, x_sample, mem_prompt, cache_k, cache_v, cache_logf, cache_mem_k, cache_mem_v,
           state_lru_h, state_lru_conv, state_ffn_conv, page_table,
           g_mix, w_in, b_f, w_o_fox, w_lru_conv, b_lru_conv, w_lru_a, b_lru_a, w_lru_x, b_lru_x,
           lru_lambda, w_o_lru, g_mem, w_mem_kv, w_o_mem, w_gate, b_gate, w_out,
           g_ffn, w_ffn_gate, w_ffn_up, w_ffn_conv, b_ffn_conv, w_ffn_down, g_final):
    depth = w_in.shape[0]
    assert depth == 1, "the final norm is fused into the single layer"
    l = 0
    B, S, D = x_prompt.shape
    DB, T, _ = x_sample.shape
    n_pool = cache_k.shape[1]
    w = _prep_weights(l, g_mix, w_in, b_f, w_o_fox, w_lru_conv, b_lru_conv, w_lru_a, b_lru_a,
                      w_lru_x, b_lru_x, lru_lambda, w_o_lru, w_o_mem, w_gate, b_gate, w_out,
                      g_ffn, w_ffn_gate, w_ffn_up, w_ffn_conv, b_ffn_conv, w_ffn_down, g_final)

    mk_p, mv_p = _mem_kv(mem_prompt, g_mem[l].reshape(1, -1), w_mem_kv[l].astype(BF16))
    (k_p, v_p, logf_p, ck_p, qb, kb, vb, olru_p, omem_p, xtail_p, htail_p) = _proj_prompt(
        x_prompt, mk_p, mv_p, w)
    ofox_p = _fox_prompt(qb, kb, vb, ck_p)
    y_p, gtail_p = _merge_ffn_prompt(x_prompt, ofox_p, olru_p, omem_p, w)

    x2 = x_sample.reshape(DB * T, D)
    cstate_tm = jnp.swapaxes(state_lru_conv[l], 0, 1)
    fstate_tm = jnp.swapaxes(state_ffn_conv[l], 0, 1)
    (q_s, k_s, v_s, logf_s, qm_s, olru_s, xtail_s, hlast_s) = _proj_sample(
        x2, cstate_tm, state_lru_h[l], w)
    omem_s = _mem_sample(qm_s, cache_mem_k[l].reshape(DB, -1, MEM_W),
                         cache_mem_v[l].reshape(DB, -1, MEM_W))
    cb, cn = _logf_cumsum(page_table, cache_logf[l], logf_s.reshape(DB, T, N_HEADS))
    ofox_s = _fox_sample(page_table, q_s, k_s, v_s, cb, cn,
                         cache_k[l].reshape(n_pool, PAGE, FOX_W),
                         cache_v[l].reshape(n_pool, PAGE, FOX_W))
    y_s, gtail_s = _merge_ffn_sample(x2, ofox_s, olru_s, omem_s, fstate_tm, w)

    heads = lambda a, n: a.reshape(1, n, -1, N_HEADS, HEAD_DIM)
    mem_heads = lambda a: a.reshape(1, B, -1, MEM_HEADS, MEM_DIM)
    return (
        y_p,
        y_s.reshape(DB, T, D),
        heads(k_p, B), heads(v_p, B), logf_p[None],
        mem_heads(mk_p), mem_heads(mv_p),
        htail_p[None, :, -1, :],
        xtail_p[None, :, SUBLANES - (CONV_LRU - 1):, :],
        gtail_p[None, :, SUBLANES - (CONV_FFN - 1):, :],
        heads(k_s, DB), heads(v_s, DB), logf_s.reshape(1, DB, T, N_HEADS),
        hlast_s[None],
        jnp.swapaxes(xtail_s, 0, 1)[None],
        jnp.swapaxes(gtail_s, 0, 1)[None],
    )
```

```python
import functools

import jax
import jax.numpy as jnp
from jax import lax
from jax.experimental import pallas as pl
from jax.experimental.pallas import tpu as pltpu

F32 = jnp.float32
BF16 = jnp.bfloat16

EPS = 1e-6
LRU_C = 8.0
NEG_BIG = -1e30
LANES = 128
SUBLANES = 8
VMEM_LIMIT = 56 * 1024 * 1024

N_HEADS = 8
HEAD_DIM = 64
FOX_W = N_HEADS * HEAD_DIM
LRU_W = 512
MEM_HEADS = 4
MEM_DIM = 128
MEM_W = MEM_HEADS * MEM_DIM
CONV_LRU = 4
CONV_FFN = 3
PAGE = 128

TM_PROJ = 256
TM_FFN = 256
TQ = 256
FF_CHUNK = 1024
PAGES_PER_STEP = 16
MEM_BATCH = 8
SAMPLE_ROWS = 256


def _rms(x, g):
    return x * lax.rsqrt(jnp.mean(x * x, axis=-1, keepdims=True) + EPS) * g


def _log_sigmoid(x):
    return jnp.minimum(x, 0.0) - jnp.log1p(jnp.exp(-jnp.abs(x)))


def _gelu(x):
    return 0.5 * x * (1.0 + jnp.tanh(0.7978845608028654 * (x + 0.044715 * (x * x * x))))


def _dot(a, b):
    return jnp.dot(a, b, preferred_element_type=F32)


def _dot_nt(a, b):
    return lax.dot_general(a, b, (((1,), (1,)), ((), ())), preferred_element_type=F32)


def _lru_coeffs(xc, wax_ref, bax_ref, lam_ref):
    gates = _dot(xc.astype(BF16), wax_ref[...]) + bax_ref[...]
    r = jax.nn.sigmoid(gates[:, :LRU_W])
    i = jax.nn.sigmoid(gates[:, LRU_W:])
    log_a = LRU_C * r * _log_sigmoid(lam_ref[...])
    a = jnp.exp(log_a)
    th = jnp.tanh(log_a)
    gated = jnp.sqrt(-2.0 * th / (1.0 - th)) * i * xc
    return a, gated


def _mem_attend_head(q, mk, mv):
    s = _dot_nt(q, mk)
    m = jnp.max(s, axis=-1, keepdims=True)
    p = jnp.exp(s - m)
    l = jnp.sum(p, axis=-1, keepdims=True)
    return _dot(p.astype(BF16), mv) / l


def _lane_cumsum(x, width):
    lane = lax.broadcasted_iota(jnp.int32, x.shape, x.ndim - 1)
    d = 1
    while d < width:
        x = x + jnp.where(lane >= d, pltpu.roll(x, d, axis=x.ndim - 1), 0.0)
        d *= 2
    return x


def _park(scr, x):
    for c in range(scr.shape[0]):
        scr[c] = x[:, c * LANES:(c + 1) * LANES]


def _unpark(scr):
    return jnp.concatenate([scr[c] for c in range(scr.shape[0])], axis=1)


def _token_rows(scr, t, nseq, ntok):
    return jnp.concatenate([scr[c, pl.ds(t, nseq, stride=ntok), :] for c in range(scr.shape[0])],
                           axis=1)


def _set_token_rows(scr, t, nseq, ntok, x):
    for c in range(scr.shape[0]):
        scr[c, pl.ds(t, nseq, stride=ntok), :] = x[:, c * LANES:(c + 1) * LANES]


def _const_spec(shape):
    nd = len(shape)
    return pl.BlockSpec(shape, lambda *_: (0,) * nd, pipeline_mode=pl.Buffered(1))


def _mem_kv_kernel(mem_ref, g_ref, w_ref, mk_ref, mv_ref):
    u = _rms(mem_ref[0], g_ref[...]).astype(BF16)
    z = _dot(u, w_ref[...])
    mk_ref[0] = z[:, :MEM_W]
    mv_ref[0] = z[:, MEM_W:]


def _mem_kv(mem, g_mem, w_mem_kv):
    B, M, D = mem.shape
    return pl.pallas_call(
        _mem_kv_kernel,
        grid=(B,),
        in_specs=[pl.BlockSpec((1, M, D), lambda b: (b, 0, 0)),
                  _const_spec((1, D)), _const_spec((D, 2 * MEM_W))],
        out_specs=[pl.BlockSpec((1, M, MEM_W), lambda b: (b, 0, 0))] * 2,
        out_shape=[jax.ShapeDtypeStruct((B, M, MEM_W), F32)] * 2,
        compiler_params=pltpu.CompilerParams(dimension_semantics=("arbitrary",),
                                             vmem_limit_bytes=VMEM_LIMIT),
        name="mem_kv",
    )(mem, g_mem, w_mem_kv)


def _proj_prompt_kernel(x_ref, gmix_ref, wqkv_ref, wf_ref, bf_ref, wrnn_ref, wqm_ref,
                        wconv_ref, bconv_ref, wax_ref, bax_ref, lam_ref, mk_ref, mv_ref,
                        k_ref, v_ref, logf_ref, ck_ref, qb_ref, kb_ref, vb_ref,
                        olru_ref, omem_ref, xtail_ref, htail_ref,
                        xext_scr, a_scr, g_scr, h_scr, c_scr):
    tm = x_ref.shape[1]
    pad = a_scr.shape[0] - tm
    t = pl.program_id(1)

    @pl.when(t == 0)
    def _():
        xext_scr[0:SUBLANES, :] = jnp.zeros((SUBLANES, LRU_W), F32)
        h_scr[...] = jnp.zeros_like(h_scr)
        c_scr[...] = jnp.zeros_like(c_scr)
        a_scr[0:pad, :] = jnp.ones((pad, LRU_W), F32)
        g_scr[0:pad, :] = jnp.zeros((pad, LRU_W), F32)

    u = _rms(x_ref[0], gmix_ref[...]).astype(BF16)

    zqkv = _dot(u, wqkv_ref[...])
    k_ref[0] = zqkv[:, FOX_W:2 * FOX_W]
    v_ref[0] = zqkv[:, 2 * FOX_W:]
    qb_ref[0] = zqkv[:, :FOX_W].astype(BF16)
    kb_ref[0] = zqkv[:, FOX_W:2 * FOX_W].astype(BF16)
    vb_ref[0] = zqkv[:, 2 * FOX_W:].astype(BF16)

    logf = _log_sigmoid(_dot(u, wf_ref[...]) + bf_ref[...])
    logf_ref[0] = logf[:, :N_HEADS]
    c = _lane_cumsum(logf.T[:N_HEADS, :], tm) + c_scr[:, 0:1]
    ck_ref[0] = c
    c_scr[...] = jnp.broadcast_to(c[:, tm - 1:tm], c_scr.shape)

    zr = _dot(u, wrnn_ref[...])
    xr = zr[:, :LRU_W]
    xext_scr[SUBLANES:SUBLANES + tm, :] = xr
    xtail_ref[0] = xr[tm - SUBLANES:, :]
    xc = bconv_ref[...] + wconv_ref[CONV_LRU - 1:CONV_LRU, :] * xr
    for j in range(CONV_LRU - 1):
        off = SUBLANES - (CONV_LRU - 1) + j
        xc = xc + wconv_ref[j:j + 1, :] * xext_scr[off:off + tm, :]
    xext_scr[0:SUBLANES, :] = xr[tm - SUBLANES:, :]

    a, gated = _lru_coeffs(xc, wax_ref, bax_ref, lam_ref)
    d = 1
    while d < tm:
        a_scr[pad:pad + tm, :] = a
        g_scr[pad:pad + tm, :] = gated
        gated = gated + a * g_scr[pad - d:pad - d + tm, :]
        a = a * a_scr[pad - d:pad - d + tm, :]
        d *= 2
    hs = gated + a * h_scr[0:1, :]
    h_scr[...] = jnp.broadcast_to(hs[tm - 1:tm, :], h_scr.shape)
    htail_ref[0] = hs[tm - SUBLANES:, :]
    olru_ref[0] = (hs * _gelu(zr[:, LRU_W:])).astype(BF16)

    zq = _dot(u, wqm_ref[...]) * (MEM_DIM ** -0.5)
    for h in range(MEM_HEADS):
        sl = slice(h * MEM_DIM, (h + 1) * MEM_DIM)
        o = _mem_attend_head(zq[:, sl].astype(BF16), mk_ref[0, :, sl].astype(BF16),
                             mv_ref[0, :, sl].astype(BF16))
        omem_ref[0, :, sl] = o.astype(BF16)


def _proj_prompt(x, mk, mv, w):
    B, S, D = x.shape
    M = mk.shape[1]
    tm = TM_PROJ
    nt = S // tm
    row = lambda width: pl.BlockSpec((1, tm, width), lambda b, t: (b, t, 0))
    per_b = lambda rows, width: pl.BlockSpec((1, rows, width), lambda b, t: (b, 0, 0))
    consts = [w["g_mix"], w["w_qkv"], w["w_f"], w["b_f"], w["w_rnn"], w["w_qm"],
              w["w_lru_conv"], w["b_lru_conv"], w["w_ax"], w["b_ax"], w["lam"]]
    out_shape = [
        jax.ShapeDtypeStruct((B, S, FOX_W), F32),
        jax.ShapeDtypeStruct((B, S, FOX_W), F32),
        jax.ShapeDtypeStruct((B, S, N_HEADS), F32),
        jax.ShapeDtypeStruct((B, N_HEADS, S), F32),
        jax.ShapeDtypeStruct((B, S, FOX_W), BF16),
        jax.ShapeDtypeStruct((B, S, FOX_W), BF16),
        jax.ShapeDtypeStruct((B, S, FOX_W), BF16),
        jax.ShapeDtypeStruct((B, S, LRU_W), BF16),
        jax.ShapeDtypeStruct((B, S, MEM_W), BF16),
        jax.ShapeDtypeStruct((B, SUBLANES, LRU_W), F32),
        jax.ShapeDtypeStruct((B, SUBLANES, LRU_W), F32),
    ]
    out_specs = [row(FOX_W), row(FOX_W), row(N_HEADS),
                 pl.BlockSpec((1, N_HEADS, tm), lambda b, t: (b, 0, t)),
                 row(FOX_W), row(FOX_W), row(FOX_W), row(LRU_W), row(MEM_W),
                 per_b(SUBLANES, LRU_W), per_b(SUBLANES, LRU_W)]
    pad = tm // 2
    return pl.pallas_call(
        _proj_prompt_kernel,
        grid=(B, nt),
        in_specs=[row(D)] + [_const_spec(c.shape) for c in consts] + [per_b(M, MEM_W)] * 2,
        out_specs=out_specs,
        out_shape=out_shape,
        scratch_shapes=[pltpu.VMEM((tm + SUBLANES, LRU_W), F32),
                        pltpu.VMEM((tm + pad, LRU_W), F32),
                        pltpu.VMEM((tm + pad, LRU_W), F32),
                        pltpu.VMEM((SUBLANES, LRU_W), F32),
                        pltpu.VMEM((N_HEADS, LANES), F32)],
        compiler_params=pltpu.CompilerParams(dimension_semantics=("arbitrary", "arbitrary"),
                                             vmem_limit_bytes=VMEM_LIMIT),
        name="proj_prompt",
    )(x, *consts, mk, mv)


def _fox_prompt_kernel(q_ref, k_ref, v_ref, ck_ref, o_ref):
    tq = q_ref.shape[1]
    i = pl.program_id(2)
    q2 = q_ref[0]
    lane = lax.broadcasted_iota(jnp.int32, q2.shape, 1)
    first = lane < HEAD_DIM
    rowi = lax.broadcasted_iota(jnp.int32, (tq, tq), 0)
    coli = lax.broadcasted_iota(jnp.int32, (tq, tq), 1)
    causal = coli <= rowi
    outs = []
    for hh in range(2):
        qh = jnp.where(first if hh == 0 else jnp.logical_not(first), q2, jnp.zeros_like(q2))

        def block(j, carry, masked):
            m, l, acc = carry
            start = pl.multiple_of(j * tq, tq)
            kb = k_ref[0, pl.ds(start, tq), :]
            vb = v_ref[0, pl.ds(start, tq), :]
            s = _dot_nt(qh, kb) - ck_ref[0, 0, hh:hh + 1, pl.ds(start, tq)]
            if masked:
                s = jnp.where(causal, s, NEG_BIG)
            m_new = jnp.maximum(m, jnp.max(s, axis=-1, keepdims=True))
            alpha = jnp.exp(m - m_new)
            p = jnp.exp(s - m_new)
            l = alpha * l + jnp.sum(p, axis=-1, keepdims=True)
            acc = alpha * acc + _dot(p.astype(BF16), vb)
            return m_new, l, acc

        init = (jnp.full((tq, 1), NEG_BIG, F32), jnp.zeros((tq, 1), F32),
                jnp.zeros((tq, LANES), F32))
        carry = lax.fori_loop(0, i, functools.partial(block, masked=False), init)
        m, l, acc = block(i, carry, True)
        outs.append(acc / l)
    o_ref[0] = jnp.where(first, outs[0], outs[1]).astype(o_ref.dtype)


def _fox_prompt(qb, kb, vb, ck):
    B, S, W = qb.shape
    npair = W // LANES
    ck4 = ck.reshape(B, npair, 2, S)
    return pl.pallas_call(
        _fox_prompt_kernel,
        grid=(B, npair, S // TQ),
        in_specs=[pl.BlockSpec((1, TQ, LANES), lambda b, h, i: (b, i, h)),
                  pl.BlockSpec((1, S, LANES), lambda b, h, i: (b, 0, h)),
                  pl.BlockSpec((1, S, LANES), lambda b, h, i: (b, 0, h)),
                  pl.BlockSpec((1, 1, 2, S), lambda b, h, i: (b, h, 0, 0))],
        out_specs=pl.BlockSpec((1, TQ, LANES), lambda b, h, i: (b, i, h)),
        out_shape=jax.ShapeDtypeStruct((B, S, W), BF16),
        compiler_params=pltpu.CompilerParams(
            dimension_semantics=("arbitrary", "arbitrary", "arbitrary"),
            vmem_limit_bytes=VMEM_LIMIT),
        name="fox_prompt",
    )(qb, kb, vb, ck4)


def _merge(x, branches, gmix_ref, wgate_ref, bgate_ref, wout_ref):
    D = x.shape[1]
    u = _rms(x, gmix_ref[...]).astype(BF16)
    merged = None
    for j, (o, wo_ref) in enumerate(branches):
        sl = slice(j * D, (j + 1) * D)
        gate = jax.nn.sigmoid(_dot(u, wgate_ref[:, sl]) + bgate_ref[:, sl])
        term = gate * _dot(o, wo_ref[...])
        merged = term if merged is None else merged + term
    return x + _dot(merged.astype(BF16), wout_ref[...])


def _merge_ffn_prompt_kernel(x_ref, ofox_ref, olru_ref, omem_ref, gmix_ref, wgate_ref, bgate_ref,
                             wofox_ref, wolru_ref, womem_ref, wout_ref, gffn_ref, wfg_ref, wfu_ref,
                             wfc_ref, bfc_ref, wfd_ref, gfin_ref,
                             y_ref, gtail_ref, gext_scr):
    tm = x_ref.shape[1]
    dff = wfg_ref.shape[1]
    t = pl.program_id(1)

    @pl.when(t == 0)
    def _():
        gext_scr[0:SUBLANES, :] = jnp.zeros((SUBLANES, dff), F32)

    h = _merge(x_ref[0], [(ofox_ref[0], wofox_ref), (olru_ref[0], wolru_ref),
                          (omem_ref[0], womem_ref)], gmix_ref, wgate_ref, bgate_ref, wout_ref)
    v2 = _rms(h, gffn_ref[...]).astype(BF16)
    y = h
    for c in range(dff // FF_CHUNK):
        sl = slice(c * FF_CHUNK, (c + 1) * FF_CHUNK)
        g = _dot(v2, wfg_ref[:, sl])
        gext_scr[SUBLANES:SUBLANES + tm, sl] = g
        gc = bfc_ref[:, sl] + wfc_ref[CONV_FFN - 1:CONV_FFN, sl] * g
        for j in range(CONV_FFN - 1):
            off = SUBLANES - (CONV_FFN - 1) + j
            gc = gc + wfc_ref[j:j + 1, sl] * gext_scr[off:off + tm, sl]
        act = (_gelu(gc) * _dot(v2, wfu_ref[:, sl])).astype(BF16)
        y = y + _dot(act, wfd_ref[sl, :])
    tail = gext_scr[tm:tm + SUBLANES, :]
    gtail_ref[0] = tail
    gext_scr[0:SUBLANES, :] = tail
    y_ref[0] = _rms(y, gfin_ref[...])


def _merge_ffn_consts(w):
    return [w["g_mix"], w["w_gate"], w["b_gate"], w["w_o_fox"], w["w_o_lru"], w["w_o_mem"],
            w["w_out"], w["g_ffn"], w["w_ffn_gate"], w["w_ffn_up"], w["w_ffn_conv"],
            w["b_ffn_conv"], w["w_ffn_down"], w["g_final"]]


def _merge_ffn_prompt(x, ofox, olru, omem, w):
    B, S, D = x.shape
    dff = w["w_ffn_gate"].shape[1]
    tm = TM_FFN
    row = lambda width: pl.BlockSpec((1, tm, width), lambda b, t: (b, t, 0))
    consts = _merge_ffn_consts(w)
    return pl.pallas_call(
        _merge_ffn_prompt_kernel,
        grid=(B, S // tm),
        in_specs=[row(D), row(FOX_W), row(LRU_W), row(MEM_W)]
                 + [_const_spec(c.shape) for c in consts],
        out_specs=[row(D), pl.BlockSpec((1, SUBLANES, dff), lambda b, t: (b, 0, 0))],
        out_shape=[jax.ShapeDtypeStruct((B, S, D), F32),
                   jax.ShapeDtypeStruct((B, SUBLANES, dff), F32)],
        scratch_shapes=[pltpu.VMEM((tm + SUBLANES, dff), F32)],
        compiler_params=pltpu.CompilerParams(dimension_semantics=("arbitrary", "arbitrary"),
                                             vmem_limit_bytes=VMEM_LIMIT),
        name="merge_ffn_prompt",
    )(x, ofox, olru, omem, *consts)


def _proj_sample_kernel(x_ref, gmix_ref, wqkv_ref, wf_ref, bf_ref, wrnn_ref, wqm_ref,
                        wconv_ref, bconv_ref, wax_ref, bax_ref, lam_ref, cstate_ref, h0_ref,
                        q_ref, k_ref, v_ref, logf_ref, qm_ref, olru_ref, xtail_ref, hlast_ref,
                        x_scr, h_scr):
    nb = h0_ref.shape[0]
    nt = x_ref.shape[0] // nb
    u = _rms(x_ref[...], gmix_ref[...]).astype(BF16)
    zqkv = _dot(u, wqkv_ref[...])
    q_ref[...] = zqkv[:, :FOX_W]
    k_ref[...] = zqkv[:, FOX_W:2 * FOX_W]
    v_ref[...] = zqkv[:, 2 * FOX_W:]
    logf = _log_sigmoid(_dot(u, wf_ref[...]) + bf_ref[...])
    logf_ref[...] = logf[:, :N_HEADS]
    qm_ref[...] = _dot(u, wqm_ref[...]) * (MEM_DIM ** -0.5)

    zr = _dot(u, wrnn_ref[...])
    _park(x_scr, zr[:, :LRU_W])
    xs = [cstate_ref[j] for j in range(CONV_LRU - 1)]
    xs += [_token_rows(x_scr, t, nb, nt) for t in range(nt)]
    h = h0_ref[...]
    for t in range(nt):
        xc = bconv_ref[...]
        for j in range(CONV_LRU):
            xc = xc + wconv_ref[j:j + 1, :] * xs[t + j]
        a, gated = _lru_coeffs(xc, wax_ref, bax_ref, lam_ref)
        h = a * h + gated
        _set_token_rows(h_scr, t, nb, nt, h)
    hlast_ref[...] = h
    for j in range(CONV_LRU - 1):
        xtail_ref[j] = xs[nt + j]
    olru_ref[...] = (_unpark(h_scr) * _gelu(zr[:, LRU_W:])).astype(BF16)


def _proj_sample(x2, cstate_tm, h0, w):
    R, D = x2.shape
    nb = h0.shape[0]
    nt = R // nb
    rb = SAMPLE_ROWS
    sb = rb // nt
    consts = [w["g_mix"], w["w_qkv"], w["w_f"], w["b_f"], w["w_rnn"], w["w_qm"],
              w["w_lru_conv"], w["b_lru_conv"], w["w_ax"], w["b_ax"], w["lam"]]
    row = lambda width: pl.BlockSpec((rb, width), lambda i: (i, 0))
    state = pl.BlockSpec((CONV_LRU - 1, sb, LRU_W), lambda i: (0, i, 0))
    seq = pl.BlockSpec((sb, LRU_W), lambda i: (i, 0))
    out_shape = [
        jax.ShapeDtypeStruct((R, FOX_W), F32),
        jax.ShapeDtypeStruct((R, FOX_W), F32),
        jax.ShapeDtypeStruct((R, FOX_W), F32),
        jax.ShapeDtypeStruct((R, N_HEADS), F32),
        jax.ShapeDtypeStruct((R, MEM_W), F32),
        jax.ShapeDtypeStruct((R, LRU_W), BF16),
        jax.ShapeDtypeStruct((CONV_LRU - 1, nb, LRU_W), F32),
        jax.ShapeDtypeStruct((nb, LRU_W), F32),
    ]
    return pl.pallas_call(
        _proj_sample_kernel,
        grid=(R // rb,),
        in_specs=[row(D)] + [_const_spec(c.shape) for c in consts] + [state, seq],
        out_specs=[row(FOX_W), row(FOX_W), row(FOX_W), row(N_HEADS), row(MEM_W), row(LRU_W),
                   state, seq],
        out_shape=out_shape,
        scratch_shapes=[pltpu.VMEM((LRU_W // LANES, rb, LANES), F32)] * 2,
        compiler_params=pltpu.CompilerParams(dimension_semantics=("arbitrary",),
                                             vmem_limit_bytes=VMEM_LIMIT),
        name="proj_sample",
    )(x2, *consts, cstate_tm, h0)


def _mem_sample_kernel(q_ref, mk_ref, mv_ref, o_ref):
    nb = mk_ref.shape[0]
    nt = q_ref.shape[0] // nb
    for b in range(nb):
        rows = slice(b * nt, (b + 1) * nt)
        for h in range(MEM_HEADS):
            sl = slice(h * MEM_DIM, (h + 1) * MEM_DIM)
            o_ref[rows, sl] = _mem_attend_head(q_ref[rows, sl].astype(BF16),
                                               mk_ref[b, :, h, :].astype(BF16),
                                               mv_ref[b, :, h, :].astype(BF16))


def _mem_sample(qm, mk, mv):
    R, W = qm.shape
    DB, M, H, Dm = mk.shape
    nt = R // DB
    nb = MEM_BATCH
    kv_spec = pl.BlockSpec((nb, M, H, Dm), lambda i: (i, 0, 0, 0))
    return pl.pallas_call(
        _mem_sample_kernel,
        grid=(DB // nb,),
        in_specs=[pl.BlockSpec((nb * nt, W), lambda i: (i, 0)), kv_spec, kv_spec],
        out_specs=pl.BlockSpec((nb * nt, W), lambda i: (i, 0)),
        out_shape=jax.ShapeDtypeStruct((R, W), F32),
        compiler_params=pltpu.CompilerParams(dimension_semantics=("arbitrary",),
                                             vmem_limit_bytes=VMEM_LIMIT),
        name="mem_sample",
    )(qm, mk, mv)


def _fox_sample_kernel(pt_ref, q_ref, kn_ref, vn_ref, lfn_ref, *refs):
    npg = PAGES_PER_STEP
    k_pages = refs[:npg]
    v_pages = refs[npg:2 * npg]
    f_pages = refs[2 * npg:3 * npg]
    o_ref, kb_scr, vb_scr, qbd_scr, m_scr, l_scr, acc_scr, c_scr, pack_scr = refs[3 * npg:]
    nt = q_ref.shape[0]
    nrow = N_HEADS * nt
    c = pl.program_id(1)
    lane = lax.broadcasted_iota(jnp.int32, (nt, FOX_W), 1)

    @pl.when(c == 0)
    def _():
        q = q_ref[...]
        for h in range(N_HEADS):
            mine = (lane >= h * HEAD_DIM) & (lane < (h + 1) * HEAD_DIM)
            qbd_scr[h * nt:(h + 1) * nt, :] = jnp.where(mine, q, 0.0).astype(BF16)
        m_scr[...] = jnp.full_like(m_scr, NEG_BIG)
        l_scr[...] = jnp.zeros_like(l_scr)
        acc_scr[...] = jnp.zeros_like(acc_scr)
        c_scr[...] = jnp.zeros_like(c_scr)

    def update(s, pv):
        m = m_scr[...]
        m_new = jnp.maximum(m, jnp.max(s, axis=-1, keepdims=True))
        alpha = jnp.exp(m - m_new)
        p = jnp.exp(s - m_new)
        l_scr[...] = alpha * l_scr[...] + jnp.sum(p, axis=-1, keepdims=True)
        acc_scr[...] = alpha * acc_scr[...] + pv(p.astype(BF16))
        m_scr[...] = m_new

    def head_rows(cb):
        return jnp.concatenate(
            [jnp.broadcast_to(cb[h:h + 1, :], (nt, cb.shape[1])) for h in range(N_HEADS)], axis=0)

    z = _lane_cumsum(jnp.concatenate([f_pages[p][0] for p in range(npg)], axis=0), PAGE)
    tiles = [z[p * N_HEADS:(p + 1) * N_HEADS, :] for p in range(npg)]
    totals = [jnp.broadcast_to(t[:, PAGE - 1:PAGE], t.shape) for t in tiles]
    off = c_scr[...]
    for p in range(npg):
        tiles[p] = tiles[p] + off
        off = off + totals[p]
    c_scr[...] = off

    for p in range(npg):
        kb_scr[:, p * PAGE:(p + 1) * PAGE] = k_pages[p][0].astype(BF16)
        vb_scr[:, p * PAGE:(p + 1) * PAGE] = v_pages[p][0].astype(BF16)
    s = _dot(qbd_scr[...], kb_scr[...]) - head_rows(jnp.concatenate(tiles, axis=1))
    update(s, lambda p: _dot_nt(p, vb_scr[...]))

    @pl.when(c == pl.num_programs(1) - 1)
    def _():
        zeros = jnp.zeros((PAGE - nt, FOX_W), F32)
        kn = jnp.concatenate([kn_ref[...], zeros], axis=0).astype(BF16)
        vn = jnp.concatenate([vn_ref[...], zeros], axis=0).astype(BF16)
        pack_scr[...] = jnp.zeros_like(pack_scr)
        pack_scr[0:nt, 0:N_HEADS] = lfn_ref[0]
        cn = _lane_cumsum(pack_scr[...].T[:N_HEADS, :], PAGE) + c_scr[...]
        s = _dot_nt(qbd_scr[...], kn) - head_rows(cn)
        tok = lax.broadcasted_iota(jnp.int32, (nrow, PAGE), 0) % nt
        key = lax.broadcasted_iota(jnp.int32, (nrow, PAGE), 1)
        update(jnp.where(key <= tok, s, NEG_BIG), lambda p: _dot(p, vn))
        acc = acc_scr[...] / l_scr[...]
        out = acc[0:nt, :]
        for h in range(1, N_HEADS):
            out = jnp.where(lane >= h * HEAD_DIM, acc[h * nt:(h + 1) * nt, :], out)
        o_ref[...] = out


def _fox_sample(page_table, q, k_new, v_new, logf_new, cache_kt, cache_vt, cache_ft):
    DB, npages = page_table.shape
    R, W = q.shape
    nt = R // DB
    npg = PAGES_PER_STEP
    nk = npg * PAGE

    def page_spec(rows, j):
        return pl.BlockSpec((1, rows, PAGE), lambda b, c, pt: (pt[b, c * npg + j], 0, 0))

    tok = pl.BlockSpec((nt, W), lambda b, c, pt: (b, 0))
    grid_spec = pltpu.PrefetchScalarGridSpec(
        num_scalar_prefetch=1,
        grid=(DB, npages // npg),
        in_specs=[tok, tok, tok, pl.BlockSpec((1, nt, N_HEADS), lambda b, c, pt: (b, 0, 0))]
                 + [page_spec(W, j) for j in range(npg)] * 2
                 + [page_spec(N_HEADS, j) for j in range(npg)],
        out_specs=tok,
        scratch_shapes=[pltpu.VMEM((W, nk), BF16), pltpu.VMEM((W, nk), BF16),
                        pltpu.VMEM((N_HEADS * nt, W), BF16),
                        pltpu.VMEM((N_HEADS * nt, 1), F32), pltpu.VMEM((N_HEADS * nt, 1), F32),
                        pltpu.VMEM((N_HEADS * nt, W), F32),
                        pltpu.VMEM((N_HEADS, PAGE), F32), pltpu.VMEM((PAGE, LANES), F32)],
    )
    return pl.pallas_call(
        _fox_sample_kernel,
        grid_spec=grid_spec,
        out_shape=jax.ShapeDtypeStruct((R, W), F32),
        compiler_params=pltpu.CompilerParams(dimension_semantics=("arbitrary", "arbitrary"),
                                             vmem_limit_bytes=VMEM_LIMIT),
        name="fox_sample",
    )(page_table, q, k_new, v_new, logf_new,
      *([cache_kt] * npg), *([cache_vt] * npg), *([cache_ft] * npg))


def _merge_ffn_sample_kernel(x_ref, ofox_ref, olru_ref, omem_ref, fstate_ref, gmix_ref, wgate_ref,
                             bgate_ref, wofox_ref, wolru_ref, womem_ref, wout_ref, gffn_ref,
                             wfg_ref, wfu_ref, wfc_ref, bfc_ref, wfd_ref, gfin_ref,
                             y_ref, gtail_ref, g_scr, gc_scr):
    nb = fstate_ref.shape[1]
    nt = x_ref.shape[0] // nb
    dff = wfg_ref.shape[1]
    h = _merge(x_ref[...], [(ofox_ref[...].astype(BF16), wofox_ref), (olru_ref[...], wolru_ref),
                            (omem_ref[...].astype(BF16), womem_ref)],
               gmix_ref, wgate_ref, bgate_ref, wout_ref)
    v2 = _rms(h, gffn_ref[...]).astype(BF16)
    y = h
    for c in range(dff // FF_CHUNK):
        sl = slice(c * FF_CHUNK, (c + 1) * FF_CHUNK)
        _park(g_scr, _dot(v2, wfg_ref[:, sl]))
        gs = [fstate_ref[j, :, sl] for j in range(CONV_FFN - 1)]
        gs += [_token_rows(g_scr, t, nb, nt) for t in range(nt)]
        for t in range(nt):
            gc = bfc_ref[:, sl]
            for j in range(CONV_FFN):
                gc = gc + wfc_ref[j:j + 1, sl] * gs[t + j]
            _set_token_rows(gc_scr, t, nb, nt, gc)
        for j in range(CONV_FFN - 1):
            gtail_ref[j, :, sl] = gs[nt + j]
        act = (_gelu(_unpark(gc_scr)) * _dot(v2, wfu_ref[:, sl])).astype(BF16)
        y = y + _dot(act, wfd_ref[sl, :])
    y_ref[...] = _rms(y, gfin_ref[...])


def _merge_ffn_sample(x2, ofox, olru, omem, fstate_tm, w):
    R, D = x2.shape
    nb = fstate_tm.shape[1]
    nt = R // nb
    dff = w["w_ffn_gate"].shape[1]
    rb = SAMPLE_ROWS
    sb = rb // nt
    consts = _merge_ffn_consts(w)
    row = lambda width: pl.BlockSpec((rb, width), lambda i: (i, 0))
    state = pl.BlockSpec((CONV_FFN - 1, sb, dff), lambda i: (0, i, 0))
    return pl.pallas_call(
        _merge_ffn_sample_kernel,
        grid=(R // rb,),
        in_specs=[row(D), row(FOX_W), row(LRU_W), row(MEM_W), state]
                 + [_const_spec(c.shape) for c in consts],
        out_specs=[row(D), state],
        out_shape=[jax.ShapeDtypeStruct((R, D), F32),
                   jax.ShapeDtypeStruct((CONV_FFN - 1, nb, dff), F32)],
        scratch_shapes=[pltpu.VMEM((FF_CHUNK // LANES, rb, LANES), F32)] * 2,
        compiler_params=pltpu.CompilerParams(dimension_semantics=("arbitrary",),
                                             vmem_limit_bytes=VMEM_LIMIT),
        name="merge_ffn_sample",
    )(x2, ofox, olru, omem, fstate_tm, *consts)


def _block_diag(wb):
    nb, bs, _ = wb.shape
    eye = jnp.eye(nb, dtype=wb.dtype)
    return (eye[:, None, :, None] * wb[:, :, None, :]).reshape(nb * bs, nb * bs)


def _prep_weights(l, g_mix, w_in, b_f, w_o_fox, w_lru_conv, b_lru_conv, w_lru_a, b_lru_a, w_lru_x,
                  b_lru_x, lru_lambda, w_o_lru, w_o_mem, w_gate, b_gate, w_out, g_ffn, w_ffn_gate,
                  w_ffn_up, w_ffn_conv, b_ffn_conv, w_ffn_down, g_final):
    wi = w_in[l]
    o_f = 3 * FOX_W
    o_r = o_f + N_HEADS
    o_m = o_r + 2 * LRU_W
    row = lambda v: v.reshape(1, -1)
    w_qkv = jnp.concatenate([wi[:, :FOX_W] * (HEAD_DIM ** -0.5), wi[:, FOX_W:o_f]], axis=1)
    return {
        "g_mix": row(g_mix[l]),
        "w_qkv": w_qkv.astype(BF16),
        "w_f": jnp.pad(wi[:, o_f:o_r], ((0, 0), (0, LANES - N_HEADS))).astype(BF16),
        "b_f": jnp.pad(row(b_f[l]), ((0, 0), (0, LANES - N_HEADS))),
        "w_rnn": wi[:, o_r:o_m].astype(BF16),
        "w_qm": wi[:, o_m:].astype(BF16),
        "w_lru_conv": w_lru_conv[l],
        "b_lru_conv": row(b_lru_conv[l]),
        "w_ax": jnp.concatenate([_block_diag(w_lru_a[l]), _block_diag(w_lru_x[l])],
                                axis=1).astype(BF16),
        "b_ax": jnp.concatenate([row(b_lru_a[l]), row(b_lru_x[l])], axis=1),
        "lam": row(lru_lambda[l]),
        "w_gate": w_gate[l].astype(BF16),
        "b_gate": row(b_gate[l]),
        "w_o_fox": w_o_fox[l].astype(BF16),
        "w_o_lru": w_o_lru[l].astype(BF16),
        "w_o_mem": w_o_mem[l].astype(BF16),
        "w_out": w_out[l].astype(BF16),
        "g_ffn": row(g_ffn[l]),
        "w_ffn_gate": w_ffn_gate[l].astype(BF16),
        "w_ffn_up": w_ffn_up[l].astype(BF16),
        "w_ffn_conv": w_ffn_conv[l],
        "b_ffn_conv": row(b_ffn_conv[l]),
        "w_ffn_down": w_ffn_down[l].astype(BF16),
        "g_final": row(g_final),
    }


def kernel(x_prompt, x_sample, mem_prompt, cache_k, cache_v, cache_logf, cache_mem_k, cache_mem_v,
           state_lru_h, state_lru_conv, state_ffn_conv, page_table,
           g_mix, w_in, b_f, w_o_fox, w_lru_conv, b_lru_conv, w_lru_a, b_lru_a, w_lru_x, b_lru_x,
           lru_lambda, w_o_lru, g_mem, w_mem_kv, w_o_mem, w_gate, b_gate, w_out,
           g_ffn, w_ffn_gate, w_ffn_up, w_ffn_conv, b_ffn_conv, w_ffn_down, g_final):
    depth = w_in.shape[0]
    assert depth == 1, "the final norm is fused into the single layer"
    l = 0
    B, S, D = x_prompt.shape
    DB, T, _ = x_sample.shape
    n_pool = cache_k.shape[1]
    w = _prep_weights(l, g_mix, w_in, b_f, w_o_fox, w_lru_conv, b_lru_conv, w_lru_a, b_lru_a,
                      w_lru_x, b_lru_x, lru_lambda, w_o_lru, w_o_mem, w_gate, b_gate, w_out,
                      g_ffn, w_ffn_gate, w_ffn_up, w_ffn_conv, b_ffn_conv, w_ffn_down, g_final)

    mk_p, mv_p = _mem_kv(mem_prompt, g_mem[l].reshape(1, -1), w_mem_kv[l].astype(BF16))
    (k_p, v_p, logf_p, ck_p, qb, kb, vb, olru_p, omem_p, xtail_p, htail_p) = _proj_prompt(
        x_prompt, mk_p, mv_p, w)
    ofox_p = _fox_prompt(qb, kb, vb, ck_p)
    y_p, gtail_p = _merge_ffn_prompt(x_prompt, ofox_p, olru_p, omem_p, w)

    x2 = x_sample.reshape(DB * T, D)
    cstate_tm = jnp.swapaxes(state_lru_conv[l], 0, 1)
    fstate_tm = jnp.swapaxes(state_ffn_conv[l], 0, 1)
    (q_s, k_s, v_s, logf_s, qm_s, olru_s, xtail_s, hlast_s) = _proj_sample(
        x2, cstate_tm, state_lru_h[l], w)
    omem_s = _mem_sample(qm_s, cache_mem_k[l], cache_mem_v[l])
    cache_kt = jnp.transpose(cache_k[l], (0, 2, 3, 1)).reshape(n_pool, FOX_W, PAGE)
    cache_vt = jnp.transpose(cache_v[l], (0, 2, 3, 1)).reshape(n_pool, FOX_W, PAGE)
    cache_ft = jnp.transpose(cache_logf[l], (0, 2, 1))
    ofox_s = _fox_sample(page_table, q_s, k_s, v_s, logf_s.reshape(DB, T, N_HEADS),
                         cache_kt, cache_vt, cache_ft)
    y_s, gtail_s = _merge_ffn_sample(x2, ofox_s, olru_s, omem_s, fstate_tm, w)

    heads = lambda a, n: a.reshape(1, n, -1, N_HEADS, HEAD_DIM)
    mem_heads = lambda a: a.reshape(1, B, -1, MEM_HEADS, MEM_DIM)
    return (
        y_p,
        y_s.reshape(DB, T, D),
        heads(k_p, B), heads(v_p, B), logf_p[None],
        mem_heads(mk_p), mem_heads(mv_p),
        htail_p[None, :, -1, :],
        xtail_p[None, :, SUBLANES - (CONV_LRU - 1):, :],
        gtail_p[None, :, SUBLANES - (CONV_FFN - 1):, :],
        heads(k_s, DB), heads(v_s, DB), logf_s.reshape(1, DB, T, N_HEADS),
        hlast_s[None],
        jnp.swapaxes(xtail_s, 0, 1)[None],
        jnp.swapaxes(gtail_s, 0, 1)[None],
    )
```

```python
import functools

import jax
import jax.numpy as jnp
from jax import lax
from jax.experimental import pallas as pl
from jax.experimental.pallas import tpu as pltpu

F32 = jnp.float32
BF16 = jnp.bfloat16

EPS = 1e-6
LRU_C = 8.0
NEG_BIG = -1e30
LANES = 128
SUBLANES = 8
VMEM_LIMIT = 56 * 1024 * 1024

N_HEADS = 8
HEAD_DIM = 64
FOX_W = N_HEADS * HEAD_DIM
LRU_W = 512
MEM_HEADS = 4
MEM_DIM = 128
MEM_W = MEM_HEADS * MEM_DIM
CONV_LRU = 4
CONV_FFN = 3
PAGE = 128

TM_PROJ = 256
TM_FFN = 256
TQ = 512
FF_CHUNK = 1024
PAGES_PER_STEP = 16
MEM_BATCH = 8
SAMPLE_ROWS = 256


def _rms(x, g):
    return x * lax.rsqrt(jnp.mean(x * x, axis=-1, keepdims=True) + EPS) * g


def _log_sigmoid(x):
    return jnp.minimum(x, 0.0) - jnp.log1p(jnp.exp(-jnp.abs(x)))


def _gelu(x):
    return 0.5 * x * (1.0 + jnp.tanh(0.7978845608028654 * (x + 0.044715 * (x * x * x))))


def _dot(a, b):
    return jnp.dot(a, b, preferred_element_type=F32)


def _dot_nt(a, b):
    return lax.dot_general(a, b, (((1,), (1,)), ((), ())), preferred_element_type=F32)


def _lru_coeffs(xc, wax_ref, bax_ref, lam_ref):
    gates = _dot(xc.astype(BF16), wax_ref[...]) + bax_ref[...]
    r = jax.nn.sigmoid(gates[:, :LRU_W])
    i = jax.nn.sigmoid(gates[:, LRU_W:])
    log_a = LRU_C * r * _log_sigmoid(lam_ref[...])
    a = jnp.exp(log_a)
    th = jnp.tanh(log_a)
    gated = jnp.sqrt(-2.0 * th / (1.0 - th)) * i * xc
    return a, gated


def _mem_attend_head(q, mk, mv):
    s = _dot_nt(q, mk)
    m = jnp.max(s, axis=-1, keepdims=True)
    p = jnp.exp(s - m)
    l = jnp.sum(p, axis=-1, keepdims=True)
    return _dot(p.astype(BF16), mv) / l


def _lane_cumsum(x, width):
    lane = lax.broadcasted_iota(jnp.int32, x.shape, x.ndim - 1)
    d = 1
    while d < width:
        x = x + jnp.where(lane >= d, pltpu.roll(x, d, axis=x.ndim - 1), 0.0)
        d *= 2
    return x


def _park(scr, x):
    for c in range(scr.shape[0]):
        scr[c] = x[:, c * LANES:(c + 1) * LANES]


def _unpark(scr):
    return jnp.concatenate([scr[c] for c in range(scr.shape[0])], axis=1)


def _token_rows(scr, t, nseq, ntok):
    return jnp.concatenate([scr[c, pl.ds(t, nseq, stride=ntok), :] for c in range(scr.shape[0])],
                           axis=1)


def _set_token_rows(scr, t, nseq, ntok, x):
    for c in range(scr.shape[0]):
        scr[c, pl.ds(t, nseq, stride=ntok), :] = x[:, c * LANES:(c + 1) * LANES]


def _const_spec(shape):
    nd = len(shape)
    return pl.BlockSpec(shape, lambda *_: (0,) * nd, pipeline_mode=pl.Buffered(1))


def _mem_kv_kernel(mem_ref, g_ref, w_ref, mk_ref, mv_ref):
    u = _rms(mem_ref[0], g_ref[...]).astype(BF16)
    z = _dot(u, w_ref[...])
    mk_ref[0] = z[:, :MEM_W]
    mv_ref[0] = z[:, MEM_W:]


def _mem_kv(mem, g_mem, w_mem_kv):
    B, M, D = mem.shape
    return pl.pallas_call(
        _mem_kv_kernel,
        grid=(B,),
        in_specs=[pl.BlockSpec((1, M, D), lambda b: (b, 0, 0)),
                  _const_spec((1, D)), _const_spec((D, 2 * MEM_W))],
        out_specs=[pl.BlockSpec((1, M, MEM_W), lambda b: (b, 0, 0))] * 2,
        out_shape=[jax.ShapeDtypeStruct((B, M, MEM_W), F32)] * 2,
        compiler_params=pltpu.CompilerParams(dimension_semantics=("arbitrary",),
                                             vmem_limit_bytes=VMEM_LIMIT),
        name="mem_kv",
    )(mem, g_mem, w_mem_kv)


def _proj_prompt_kernel(x_ref, gmix_ref, wqkv_ref, wf_ref, bf_ref, wrnn_ref, wqm_ref,
                        wconv_ref, bconv_ref, wax_ref, bax_ref, lam_ref, mk_ref, mv_ref,
                        k_ref, v_ref, logf_ref, ck_ref, qb_ref, kb_ref, vb_ref,
                        olru_ref, omem_ref, xtail_ref, htail_ref,
                        xext_scr, a_scr, g_scr, h_scr, c_scr):
    tm = x_ref.shape[1]
    pad = a_scr.shape[0] - tm
    t = pl.program_id(1)

    @pl.when(t == 0)
    def _():
        xext_scr[0:SUBLANES, :] = jnp.zeros((SUBLANES, LRU_W), F32)
        h_scr[...] = jnp.zeros_like(h_scr)
        c_scr[...] = jnp.zeros_like(c_scr)
        a_scr[0:pad, :] = jnp.ones((pad, LRU_W), F32)
        g_scr[0:pad, :] = jnp.zeros((pad, LRU_W), F32)

    u = _rms(x_ref[0], gmix_ref[...]).astype(BF16)

    zqkv = _dot(u, wqkv_ref[...])
    k_ref[0] = zqkv[:, FOX_W:2 * FOX_W]
    v_ref[0] = zqkv[:, 2 * FOX_W:]
    qb_ref[0] = zqkv[:, :FOX_W].astype(BF16)
    kb_ref[0] = zqkv[:, FOX_W:2 * FOX_W].astype(BF16)
    vb_ref[0] = zqkv[:, 2 * FOX_W:].astype(BF16)

    logf = _log_sigmoid(_dot(u, wf_ref[...]) + bf_ref[...])
    logf_ref[0] = logf[:, :N_HEADS]
    c = _lane_cumsum(logf.T[:N_HEADS, :], tm) + c_scr[:, 0:1]
    ck_ref[0] = c
    c_scr[...] = jnp.broadcast_to(c[:, tm - 1:tm], c_scr.shape)

    zr = _dot(u, wrnn_ref[...])
    xr = zr[:, :LRU_W]
    xext_scr[SUBLANES:SUBLANES + tm, :] = xr
    xtail_ref[0] = xr[tm - SUBLANES:, :]
    xc = bconv_ref[...] + wconv_ref[CONV_LRU - 1:CONV_LRU, :] * xr
    for j in range(CONV_LRU - 1):
        off = SUBLANES - (CONV_LRU - 1) + j
        xc = xc + wconv_ref[j:j + 1, :] * xext_scr[off:off + tm, :]
    xext_scr[0:SUBLANES, :] = xr[tm - SUBLANES:, :]

    a, gated = _lru_coeffs(xc, wax_ref, bax_ref, lam_ref)
    d = 1
    while d < tm:
        a_scr[pad:pad + tm, :] = a
        g_scr[pad:pad + tm, :] = gated
        gated = gated + a * g_scr[pad - d:pad - d + tm, :]
        a = a * a_scr[pad - d:pad - d + tm, :]
        d *= 2
    hs = gated + a * h_scr[0:1, :]
    h_scr[...] = jnp.broadcast_to(hs[tm - 1:tm, :], h_scr.shape)
    htail_ref[0] = hs[tm - SUBLANES:, :]
    olru_ref[0] = (hs * _gelu(zr[:, LRU_W:])).astype(BF16)

    zq = _dot(u, wqm_ref[...]) * (MEM_DIM ** -0.5)
    for h in range(MEM_HEADS):
        sl = slice(h * MEM_DIM, (h + 1) * MEM_DIM)
        o = _mem_attend_head(zq[:, sl].astype(BF16), mk_ref[0, :, sl].astype(BF16),
                             mv_ref[0, :, sl].astype(BF16))
        omem_ref[0, :, sl] = o.astype(BF16)


def _proj_prompt(x, mk, mv, w):
    B, S, D = x.shape
    M = mk.shape[1]
    tm = TM_PROJ
    nt = S // tm
    row = lambda width: pl.BlockSpec((1, tm, width), lambda b, t: (b, t, 0))
    per_b = lambda rows, width: pl.BlockSpec((1, rows, width), lambda b, t: (b, 0, 0))
    consts = [w["g_mix"], w["w_qkv"], w["w_f"], w["b_f"], w["w_rnn"], w["w_qm"],
              w["w_lru_conv"], w["b_lru_conv"], w["w_ax"], w["b_ax"], w["lam"]]
    out_shape = [
        jax.ShapeDtypeStruct((B, S, FOX_W), F32),
        jax.ShapeDtypeStruct((B, S, FOX_W), F32),
        jax.ShapeDtypeStruct((B, S, N_HEADS), F32),
        jax.ShapeDtypeStruct((B, N_HEADS, S), F32),
        jax.ShapeDtypeStruct((B, S, FOX_W), BF16),
        jax.ShapeDtypeStruct((B, S, FOX_W), BF16),
        jax.ShapeDtypeStruct((B, S, FOX_W), BF16),
        jax.ShapeDtypeStruct((B, S, LRU_W), BF16),
        jax.ShapeDtypeStruct((B, S, MEM_W), BF16),
        jax.ShapeDtypeStruct((B, SUBLANES, LRU_W), F32),
        jax.ShapeDtypeStruct((B, SUBLANES, LRU_W), F32),
    ]
    out_specs = [row(FOX_W), row(FOX_W), row(N_HEADS),
                 pl.BlockSpec((1, N_HEADS, tm), lambda b, t: (b, 0, t)),
                 row(FOX_W), row(FOX_W), row(FOX_W), row(LRU_W), row(MEM_W),
                 per_b(SUBLANES, LRU_W), per_b(SUBLANES, LRU_W)]
    pad = tm // 2
    return pl.pallas_call(
        _proj_prompt_kernel,
        grid=(B, nt),
        in_specs=[row(D)] + [_const_spec(c.shape) for c in consts] + [per_b(M, MEM_W)] * 2,
        out_specs=out_specs,
        out_shape=out_shape,
        scratch_shapes=[pltpu.VMEM((tm + SUBLANES, LRU_W), F32),
                        pltpu.VMEM((tm + pad, LRU_W), F32),
                        pltpu.VMEM((tm + pad, LRU_W), F32),
                        pltpu.VMEM((SUBLANES, LRU_W), F32),
                        pltpu.VMEM((N_HEADS, LANES), F32)],
        compiler_params=pltpu.CompilerParams(dimension_semantics=("arbitrary", "arbitrary"),
                                             vmem_limit_bytes=VMEM_LIMIT),
        name="proj_prompt",
    )(x, *consts, mk, mv)


def _fox_prompt_kernel(q_ref, k_ref, v_ref, ck_ref, o_ref, s_scr, p_scr, m_scr, acc_scr):
    tq = q_ref.shape[1]
    i = pl.program_id(2)
    q2 = q_ref[0]
    first = lax.broadcasted_iota(jnp.int32, q2.shape, 1) < HEAD_DIM
    zero = jnp.zeros_like(q2)
    qs = jnp.concatenate([jnp.where(first, q2, zero), jnp.where(first, zero, q2)], axis=0)
    ones = jnp.ones((tq, LANES), BF16)

    def logits(j):
        return _dot_nt(qs, k_ref[0, pl.ds(pl.multiple_of(j * tq, tq), tq), :])

    def weighted_values(p, j):
        v_aug = jnp.concatenate([v_ref[0, pl.ds(pl.multiple_of(j * tq, tq), tq), :], ones], axis=1)
        return _dot(p, v_aug)

    def softmax_block(j, slot, pv, masked):
        start = pl.multiple_of(j * tq, tq)
        if masked:
            causal = (lax.broadcasted_iota(jnp.int32, (tq, tq), 1)
                      <= lax.broadcasted_iota(jnp.int32, (tq, tq), 0))
        for hh in range(2):
            rows = slice(hh * tq, (hh + 1) * tq)
            s = s_scr[slot, rows, :] - ck_ref[0, 0, hh:hh + 1, pl.ds(start, tq)]
            if masked:
                s = jnp.where(causal, s, NEG_BIG)
            m_old = m_scr[rows, :]
            m_new = jnp.maximum(m_old, jnp.max(s, axis=-1, keepdims=True))
            p_scr[slot, rows, :] = jnp.exp(s - m_new).astype(BF16)
            acc_scr[rows, :] = jnp.exp(m_old - m_new) * (acc_scr[rows, :] + pv[rows, :])
            m_scr[rows, :] = m_new

    def step(t, slot):
        other = 1 - slot
        pv = weighted_values(p_scr[other], jnp.maximum(t - 1, 0))
        s_scr[other] = logits(t + 1)
        softmax_block(t, slot, pv, False)

    def finish(slot):
        pv = weighted_values(p_scr[1 - slot], jnp.maximum(i - 1, 0))
        softmax_block(i, slot, pv, True)
        acc = acc_scr[...] + weighted_values(p_scr[slot], i)
        out = acc[:, :LANES] / acc[:, LANES:]
        o_ref[0] = jnp.where(first, out[:tq, :], out[tq:, :]).astype(o_ref.dtype)

    m_scr[...] = jnp.full_like(m_scr, NEG_BIG)
    acc_scr[...] = jnp.zeros_like(acc_scr)
    p_scr[1] = jnp.zeros(p_scr.shape[1:], BF16)
    s_scr[0] = logits(0)

    def two_steps(tt, carry):
        step(2 * tt, 0)
        step(2 * tt + 1, 1)
        return carry

    lax.fori_loop(0, i // 2, two_steps, 0)

    @pl.when(i % 2 == 0)
    def _():
        finish(0)

    @pl.when(i % 2 == 1)
    def _():
        step(i - 1, 0)
        finish(1)


def _fox_prompt(qb, kb, vb, ck):
    B, S, W = qb.shape
    npair = W // LANES
    ck4 = ck.reshape(B, npair, 2, S)
    return pl.pallas_call(
        _fox_prompt_kernel,
        grid=(B, npair, S // TQ),
        in_specs=[pl.BlockSpec((1, TQ, LANES), lambda b, h, i: (b, i, h)),
                  pl.BlockSpec((1, S, LANES), lambda b, h, i: (b, 0, h)),
                  pl.BlockSpec((1, S, LANES), lambda b, h, i: (b, 0, h)),
                  pl.BlockSpec((1, 1, 2, S), lambda b, h, i: (b, h, 0, 0))],
        out_specs=pl.BlockSpec((1, TQ, LANES), lambda b, h, i: (b, i, h)),
        out_shape=jax.ShapeDtypeStruct((B, S, W), BF16),
        scratch_shapes=[pltpu.VMEM((2, 2 * TQ, TQ), F32), pltpu.VMEM((2, 2 * TQ, TQ), BF16),
                        pltpu.VMEM((2 * TQ, 1), F32), pltpu.VMEM((2 * TQ, 2 * LANES), F32)],
        compiler_params=pltpu.CompilerParams(
            dimension_semantics=("arbitrary", "arbitrary", "arbitrary"),
            vmem_limit_bytes=VMEM_LIMIT),
        name="fox_prompt",
    )(qb, kb, vb, ck4)


def _merge(x, branches, gmix_ref, wgate_ref, bgate_ref, wout_ref):
    D = x.shape[1]
    u = _rms(x, gmix_ref[...]).astype(BF16)
    merged = None
    for j, (o, wo_ref) in enumerate(branches):
        sl = slice(j * D, (j + 1) * D)
        gate = jax.nn.sigmoid(_dot(u, wgate_ref[:, sl]) + bgate_ref[:, sl])
        term = gate * _dot(o, wo_ref[...])
        merged = term if merged is None else merged + term
    return x + _dot(merged.astype(BF16), wout_ref[...])


def _merge_ffn_prompt_kernel(x_ref, ofox_ref, olru_ref, omem_ref, gmix_ref, wgate_ref, bgate_ref,
                             wofox_ref, wolru_ref, womem_ref, wout_ref, gffn_ref, wfg_ref, wfu_ref,
                             wfc_ref, bfc_ref, wfd_ref, gfin_ref,
                             y_ref, gtail_ref, gext_scr):
    tm = x_ref.shape[1]
    dff = wfg_ref.shape[1]
    t = pl.program_id(1)

    @pl.when(t == 0)
    def _():
        gext_scr[0:SUBLANES, :] = jnp.zeros((SUBLANES, dff), F32)

    h = _merge(x_ref[0], [(ofox_ref[0], wofox_ref), (olru_ref[0], wolru_ref),
                          (omem_ref[0], womem_ref)], gmix_ref, wgate_ref, bgate_ref, wout_ref)
    v2 = _rms(h, gffn_ref[...]).astype(BF16)
    y = h
    for c in range(dff // FF_CHUNK):
        sl = slice(c * FF_CHUNK, (c + 1) * FF_CHUNK)
        g = _dot(v2, wfg_ref[:, sl])
        gext_scr[SUBLANES:SUBLANES + tm, sl] = g
        gc = bfc_ref[:, sl] + wfc_ref[CONV_FFN - 1:CONV_FFN, sl] * g
        for j in range(CONV_FFN - 1):
            off = SUBLANES - (CONV_FFN - 1) + j
            gc = gc + wfc_ref[j:j + 1, sl] * gext_scr[off:off + tm, sl]
        act = (_gelu(gc) * _dot(v2, wfu_ref[:, sl])).astype(BF16)
        y = y + _dot(act, wfd_ref[sl, :])
    tail = gext_scr[tm:tm + SUBLANES, :]
    gtail_ref[0] = tail
    gext_scr[0:SUBLANES, :] = tail
    y_ref[0] = _rms(y, gfin_ref[...])


def _merge_ffn_consts(w):
    return [w["g_mix"], w["w_gate"], w["b_gate"], w["w_o_fox"], w["w_o_lru"], w["w_o_mem"],
            w["w_out"], w["g_ffn"], w["w_ffn_gate"], w["w_ffn_up"], w["w_ffn_conv"],
            w["b_ffn_conv"], w["w_ffn_down"], w["g_final"]]


def _merge_ffn_prompt(x, ofox, olru, omem, w):
    B, S, D = x.shape
    dff = w["w_ffn_gate"].shape[1]
    tm = TM_FFN
    row = lambda width: pl.BlockSpec((1, tm, width), lambda b, t: (b, t, 0))
    consts = _merge_ffn_consts(w)
    return pl.pallas_call(
        _merge_ffn_prompt_kernel,
        grid=(B, S // tm),
        in_specs=[row(D), row(FOX_W), row(LRU_W), row(MEM_W)]
                 + [_const_spec(c.shape) for c in consts],
        out_specs=[row(D), pl.BlockSpec((1, SUBLANES, dff), lambda b, t: (b, 0, 0))],
        out_shape=[jax.ShapeDtypeStruct((B, S, D), F32),
                   jax.ShapeDtypeStruct((B, SUBLANES, dff), F32)],
        scratch_shapes=[pltpu.VMEM((tm + SUBLANES, dff), F32)],
        compiler_params=pltpu.CompilerParams(dimension_semantics=("arbitrary", "arbitrary"),
                                             vmem_limit_bytes=VMEM_LIMIT),
        name="merge_ffn_prompt",
    )(x, ofox, olru, omem, *consts)


def _proj_sample_kernel(x_ref, gmix_ref, wqkv_ref, wf_ref, bf_ref, wrnn_ref, wqm_ref,
                        wconv_ref, bconv_ref, wax_ref, bax_ref, lam_ref, cstate_ref, h0_ref,
                        q_ref, k_ref, v_ref, logf_ref, qm_ref, olru_ref, xtail_ref, hlast_ref,
                        x_scr, h_scr):
    nb = h0_ref.shape[0]
    nt = x_ref.shape[0] // nb
    u = _rms(x_ref[...], gmix_ref[...]).astype(BF16)
    zqkv = _dot(u, wqkv_ref[...])
    q_ref[...] = zqkv[:, :FOX_W]
    k_ref[...] = zqkv[:, FOX_W:2 * FOX_W]
    v_ref[...] = zqkv[:, 2 * FOX_W:]
    logf = _log_sigmoid(_dot(u, wf_ref[...]) + bf_ref[...])
    logf_ref[...] = logf[:, :N_HEADS]
    qm_ref[...] = _dot(u, wqm_ref[...]) * (MEM_DIM ** -0.5)

    zr = _dot(u, wrnn_ref[...])
    _park(x_scr, zr[:, :LRU_W])
    xs = [cstate_ref[j] for j in range(CONV_LRU - 1)]
    xs += [_token_rows(x_scr, t, nb, nt) for t in range(nt)]
    h = h0_ref[...]
    for t in range(nt):
        xc = bconv_ref[...]
        for j in range(CONV_LRU):
            xc = xc + wconv_ref[j:j + 1, :] * xs[t + j]
        a, gated = _lru_coeffs(xc, wax_ref, bax_ref, lam_ref)
        h = a * h + gated
        _set_token_rows(h_scr, t, nb, nt, h)
    hlast_ref[...] = h
    for j in range(CONV_LRU - 1):
        xtail_ref[j] = xs[nt + j]
    olru_ref[...] = (_unpark(h_scr) * _gelu(zr[:, LRU_W:])).astype(BF16)


def _proj_sample(x2, cstate_tm, h0, w):
    R, D = x2.shape
    nb = h0.shape[0]
    nt = R // nb
    rb = SAMPLE_ROWS
    sb = rb // nt
    consts = [w["g_mix"], w["w_qkv"], w["w_f"], w["b_f"], w["w_rnn"], w["w_qm"],
              w["w_lru_conv"], w["b_lru_conv"], w["w_ax"], w["b_ax"], w["lam"]]
    row = lambda width: pl.BlockSpec((rb, width), lambda i: (i, 0))
    state = pl.BlockSpec((CONV_LRU - 1, sb, LRU_W), lambda i: (0, i, 0))
    seq = pl.BlockSpec((sb, LRU_W), lambda i: (i, 0))
    out_shape = [
        jax.ShapeDtypeStruct((R, FOX_W), F32),
        jax.ShapeDtypeStruct((R, FOX_W), F32),
        jax.ShapeDtypeStruct((R, FOX_W), F32),
        jax.ShapeDtypeStruct((R, N_HEADS), F32),
        jax.ShapeDtypeStruct((R, MEM_W), F32),
        jax.ShapeDtypeStruct((R, LRU_W), BF16),
        jax.ShapeDtypeStruct((CONV_LRU - 1, nb, LRU_W), F32),
        jax.ShapeDtypeStruct((nb, LRU_W), F32),
    ]
    return pl.pallas_call(
        _proj_sample_kernel,
        grid=(R // rb,),
        in_specs=[row(D)] + [_const_spec(c.shape) for c in consts] + [state, seq],
        out_specs=[row(FOX_W), row(FOX_W), row(FOX_W), row(N_HEADS), row(MEM_W), row(LRU_W),
                   state, seq],
        out_shape=out_shape,
        scratch_shapes=[pltpu.VMEM((LRU_W // LANES, rb, LANES), F32)] * 2,
        compiler_params=pltpu.CompilerParams(dimension_semantics=("arbitrary",),
                                             vmem_limit_bytes=VMEM_LIMIT),
        name="proj_sample",
    )(x2, *consts, cstate_tm, h0)


def _mem_sample_kernel(q_ref, mk_ref, mv_ref, o_ref):
    nb = mk_ref.shape[0]
    nt = q_ref.shape[0] // nb
    for b in range(nb):
        rows = slice(b * nt, (b + 1) * nt)
        for h in range(MEM_HEADS):
            sl = slice(h * MEM_DIM, (h + 1) * MEM_DIM)
            o_ref[rows, sl] = _mem_attend_head(q_ref[rows, sl].astype(BF16),
                                               mk_ref[b, :, h, :].astype(BF16),
                                               mv_ref[b, :, h, :].astype(BF16))


def _mem_sample(qm, mk, mv):
    R, W = qm.shape
    DB, M, H, Dm = mk.shape
    nt = R // DB
    nb = MEM_BATCH
    kv_spec = pl.BlockSpec((nb, M, H, Dm), lambda i: (i, 0, 0, 0))
    return pl.pallas_call(
        _mem_sample_kernel,
        grid=(DB // nb,),
        in_specs=[pl.BlockSpec((nb * nt, W), lambda i: (i, 0)), kv_spec, kv_spec],
        out_specs=pl.BlockSpec((nb * nt, W), lambda i: (i, 0)),
        out_shape=jax.ShapeDtypeStruct((R, W), F32),
        compiler_params=pltpu.CompilerParams(dimension_semantics=("arbitrary",),
                                             vmem_limit_bytes=VMEM_LIMIT),
        name="mem_sample",
    )(qm, mk, mv)


def _fox_sample_kernel(pt_ref, q_ref, kn_ref, vn_ref, lfn_ref, *refs):
    npg = PAGES_PER_STEP
    k_pages = refs[:npg]
    v_pages = refs[npg:2 * npg]
    f_pages = refs[2 * npg:3 * npg]
    o_ref, kb_scr, vb_scr, qbd_scr, m_scr, l_scr, acc_scr, c_scr, pack_scr = refs[3 * npg:]
    nt = q_ref.shape[0]
    nrow = N_HEADS * nt
    c = pl.program_id(1)
    lane = lax.broadcasted_iota(jnp.int32, (nt, FOX_W), 1)

    @pl.when(c == 0)
    def _():
        q = q_ref[...]
        for h in range(N_HEADS):
            mine = (lane >= h * HEAD_DIM) & (lane < (h + 1) * HEAD_DIM)
            qbd_scr[h * nt:(h + 1) * nt, :] = jnp.where(mine, q, 0.0).astype(BF16)
        m_scr[...] = jnp.full_like(m_scr, NEG_BIG)
        l_scr[...] = jnp.zeros_like(l_scr)
        acc_scr[...] = jnp.zeros_like(acc_scr)
        c_scr[...] = jnp.zeros_like(c_scr)

    def update(s, pv):
        m = m_scr[...]
        m_new = jnp.maximum(m, jnp.max(s, axis=-1, keepdims=True))
        alpha = jnp.exp(m - m_new)
        p = jnp.exp(s - m_new)
        l_scr[...] = alpha * l_scr[...] + jnp.sum(p, axis=-1, keepdims=True)
        acc_scr[...] = alpha * acc_scr[...] + pv(p.astype(BF16))
        m_scr[...] = m_new

    def head_rows(cb):
        return jnp.concatenate(
            [jnp.broadcast_to(cb[h:h + 1, :], (nt, cb.shape[1])) for h in range(N_HEADS)], axis=0)

    z = _lane_cumsum(jnp.concatenate([f_pages[p][0] for p in range(npg)], axis=0), PAGE)
    tiles = [z[p * N_HEADS:(p + 1) * N_HEADS, :] for p in range(npg)]
    totals = [jnp.broadcast_to(t[:, PAGE - 1:PAGE], t.shape) for t in tiles]
    off = c_scr[...]
    for p in range(npg):
        tiles[p] = tiles[p] + off
        off = off + totals[p]
    c_scr[...] = off

    for p in range(npg):
        kb_scr[:, p * PAGE:(p + 1) * PAGE] = k_pages[p][0].astype(BF16)
        vb_scr[:, p * PAGE:(p + 1) * PAGE] = v_pages[p][0].astype(BF16)
    s = _dot(qbd_scr[...], kb_scr[...]) - head_rows(jnp.concatenate(tiles, axis=1))
    update(s, lambda p: _dot_nt(p, vb_scr[...]))

    @pl.when(c == pl.num_programs(1) - 1)
    def _():
        zeros = jnp.zeros((PAGE - nt, FOX_W), F32)
        kn = jnp.concatenate([kn_ref[...], zeros], axis=0).astype(BF16)
        vn = jnp.concatenate([vn_ref[...], zeros], axis=0).astype(BF16)
        pack_scr[...] = jnp.zeros_like(pack_scr)
        pack_scr[0:nt, 0:N_HEADS] = lfn_ref[0]
        cn = _lane_cumsum(pack_scr[...].T[:N_HEADS, :], PAGE) + c_scr[...]
        s = _dot_nt(qbd_scr[...], kn) - head_rows(cn)
        tok = lax.broadcasted_iota(jnp.int32, (nrow, PAGE), 0) % nt
        key = lax.broadcasted_iota(jnp.int32, (nrow, PAGE), 1)
        update(jnp.where(key <= tok, s, NEG_BIG), lambda p: _dot(p, vn))
        acc = acc_scr[...] / l_scr[...]
        out = acc[0:nt, :]
        for h in range(1, N_HEADS):
            out = jnp.where(lane >= h * HEAD_DIM, acc[h * nt:(h + 1) * nt, :], out)
        o_ref[...] = out


def _fox_sample(page_table, q, k_new, v_new, logf_new, cache_kt, cache_vt, cache_ft):
    DB, npages = page_table.shape
    R, W = q.shape
    nt = R // DB
    npg = PAGES_PER_STEP
    nk = npg * PAGE

    def page_spec(rows, j):
        return pl.BlockSpec((1, rows, PAGE), lambda b, c, pt: (pt[b, c * npg + j], 0, 0))

    tok = pl.BlockSpec((nt, W), lambda b, c, pt: (b, 0))
    grid_spec = pltpu.PrefetchScalarGridSpec(
        num_scalar_prefetch=1,
        grid=(DB, npages // npg),
        in_specs=[tok, tok, tok, pl.BlockSpec((1, nt, N_HEADS), lambda b, c, pt: (b, 0, 0))]
                 + [page_spec(W, j) for j in range(npg)] * 2
                 + [page_spec(N_HEADS, j) for j in range(npg)],
        out_specs=tok,
        scratch_shapes=[pltpu.VMEM((W, nk), BF16), pltpu.VMEM((W, nk), BF16),
                        pltpu.VMEM((N_HEADS * nt, W), BF16),
                        pltpu.VMEM((N_HEADS * nt, 1), F32), pltpu.VMEM((N_HEADS * nt, 1), F32),
                        pltpu.VMEM((N_HEADS * nt, W), F32),
                        pltpu.VMEM((N_HEADS, PAGE), F32), pltpu.VMEM((PAGE, LANES), F32)],
    )
    return pl.pallas_call(
        _fox_sample_kernel,
        grid_spec=grid_spec,
        out_shape=jax.ShapeDtypeStruct((R, W), F32),
        compiler_params=pltpu.CompilerParams(dimension_semantics=("arbitrary", "arbitrary"),
                                             vmem_limit_bytes=VMEM_LIMIT),
        name="fox_sample",
    )(page_table, q, k_new, v_new, logf_new,
      *([cache_kt] * npg), *([cache_vt] * npg), *([cache_ft] * npg))


def _merge_ffn_sample_kernel(x_ref, ofox_ref, olru_ref, omem_ref, fstate_ref, gmix_ref, wgate_ref,
                             bgate_ref, wofox_ref, wolru_ref, womem_ref, wout_ref, gffn_ref,
                             wfg_ref, wfu_ref, wfc_ref, bfc_ref, wfd_ref, gfin_ref,
                             y_ref, gtail_ref, g_scr, gc_scr):
    nb = fstate_ref.shape[1]
    nt = x_ref.shape[0] // nb
    dff = wfg_ref.shape[1]
    h = _merge(x_ref[...], [(ofox_ref[...].astype(BF16), wofox_ref), (olru_ref[...], wolru_ref),
                            (omem_ref[...].astype(BF16), womem_ref)],
               gmix_ref, wgate_ref, bgate_ref, wout_ref)
    v2 = _rms(h, gffn_ref[...]).astype(BF16)
    y = h
    for c in range(dff // FF_CHUNK):
        sl = slice(c * FF_CHUNK, (c + 1) * FF_CHUNK)
        _park(g_scr, _dot(v2, wfg_ref[:, sl]))
        gs = [fstate_ref[j, :, sl] for j in range(CONV_FFN - 1)]
        gs += [_token_rows(g_scr, t, nb, nt) for t in range(nt)]
        for t in range(nt):
            gc = bfc_ref[:, sl]
            for j in range(CONV_FFN):
                gc = gc + wfc_ref[j:j + 1, sl] * gs[t + j]
            _set_token_rows(gc_scr, t, nb, nt, gc)
        for j in range(CONV_FFN - 1):
            gtail_ref[j, :, sl] = gs[nt + j]
        act = (_gelu(_unpark(gc_scr)) * _dot(v2, wfu_ref[:, sl])).astype(BF16)
        y = y + _dot(act, wfd_ref[sl, :])
    y_ref[...] = _rms(y, gfin_ref[...])


def _merge_ffn_sample(x2, ofox, olru, omem, fstate_tm, w):
    R, D = x2.shape
    nb = fstate_tm.shape[1]
    nt = R // nb
    dff = w["w_ffn_gate"].shape[1]
    rb = SAMPLE_ROWS
    sb = rb // nt
    consts = _merge_ffn_consts(w)
    row = lambda width: pl.BlockSpec((rb, width), lambda i: (i, 0))
    state = pl.BlockSpec((CONV_FFN - 1, sb, dff), lambda i: (0, i, 0))
    return pl.pallas_call(
        _merge_ffn_sample_kernel,
        grid=(R // rb,),
        in_specs=[row(D), row(FOX_W), row(LRU_W), row(MEM_W), state]
                 + [_const_spec(c.shape) for c in consts],
        out_specs=[row(D), state],
        out_shape=[jax.ShapeDtypeStruct((R, D), F32),
                   jax.ShapeDtypeStruct((CONV_FFN - 1, nb, dff), F32)],
        scratch_shapes=[pltpu.VMEM((FF_CHUNK // LANES, rb, LANES), F32)] * 2,
        compiler_params=pltpu.CompilerParams(dimension_semantics=("arbitrary",),
                                             vmem_limit_bytes=VMEM_LIMIT),
        name="merge_ffn_sample",
    )(x2, ofox, olru, omem, fstate_tm, *consts)


def _block_diag(wb):
    nb, bs, _ = wb.shape
    eye = jnp.eye(nb, dtype=wb.dtype)
    return (eye[:, None, :, None] * wb[:, :, None, :]).reshape(nb * bs, nb * bs)


def _prep_weights(l, g_mix, w_in, b_f, w_o_fox, w_lru_conv, b_lru_conv, w_lru_a, b_lru_a, w_lru_x,
                  b_lru_x, lru_lambda, w_o_lru, w_o_mem, w_gate, b_gate, w_out, g_ffn, w_ffn_gate,
                  w_ffn_up, w_ffn_conv, b_ffn_conv, w_ffn_down, g_final):
    wi = w_in[l]
    o_f = 3 * FOX_W
    o_r = o_f + N_HEADS
    o_m = o_r + 2 * LRU_W
    row = lambda v: v.reshape(1, -1)
    w_qkv = jnp.concatenate([wi[:, :FOX_W] * (HEAD_DIM ** -0.5), wi[:, FOX_W:o_f]], axis=1)
    return {
        "g_mix": row(g_mix[l]),
        "w_qkv": w_qkv.astype(BF16),
        "w_f": jnp.pad(wi[:, o_f:o_r], ((0, 0), (0, LANES - N_HEADS))).astype(BF16),
        "b_f": jnp.pad(row(b_f[l]), ((0, 0), (0, LANES - N_HEADS))),
        "w_rnn": wi[:, o_r:o_m].astype(BF16),
        "w_qm": wi[:, o_m:].astype(BF16),
        "w_lru_conv": w_lru_conv[l],
        "b_lru_conv": row(b_lru_conv[l]),
        "w_ax": jnp.concatenate([_block_diag(w_lru_a[l]), _block_diag(w_lru_x[l])],
                                axis=1).astype(BF16),
        "b_ax": jnp.concatenate([row(b_lru_a[l]), row(b_lru_x[l])], axis=1),
        "lam": row(lru_lambda[l]),
        "w_gate": w_gate[l].astype(BF16),
        "b_gate": row(b_gate[l]),
        "w_o_fox": w_o_fox[l].astype(BF16),
        "w_o_lru": w_o_lru[l].astype(BF16),
        "w_o_mem": w_o_mem[l].astype(BF16),
        "w_out": w_out[l].astype(BF16),
        "g_ffn": row(g_ffn[l]),
        "w_ffn_gate": w_ffn_gate[l].astype(BF16),
        "w_ffn_up": w_ffn_up[l].astype(BF16),
        "w_ffn_conv": w_ffn_conv[l],
        "b_ffn_conv": row(b_ffn_conv[l]),
        "w_ffn_down": w_ffn_down[l].astype(BF16),
        "g_final": row(g_final),
    }


def kernel(x_prompt, x_sample, mem_prompt, cache_k, cache_v, cache_logf, cache_mem_k, cache_mem_v,
           state_lru_h, state_lru_conv, state_ffn_conv, page_table,
           g_mix, w_in, b_f, w_o_fox, w_lru_conv, b_lru_conv, w_lru_a, b_lru_a, w_lru_x, b_lru_x,
           lru_lambda, w_o_lru, g_mem, w_mem_kv, w_o_mem, w_gate, b_gate, w_out,
           g_ffn, w_ffn_gate, w_ffn_up, w_ffn_conv, b_ffn_conv, w_ffn_down, g_final):
    depth = w_in.shape[0]
    assert depth == 1, "the final norm is fused into the single layer"
    l = 0
    B, S, D = x_prompt.shape
    DB, T, _ = x_sample.shape
    n_pool = cache_k.shape[1]
    w = _prep_weights(l, g_mix, w_in, b_f, w_o_fox, w_lru_conv, b_lru_conv, w_lru_a, b_lru_a,
                      w_lru_x, b_lru_x, lru_lambda, w_o_lru, w_o_mem, w_gate, b_gate, w_out,
                      g_ffn, w_ffn_gate, w_ffn_up, w_ffn_conv, b_ffn_conv, w_ffn_down, g_final)

    mk_p, mv_p = _mem_kv(mem_prompt, g_mem[l].reshape(1, -1), w_mem_kv[l].astype(BF16))
    (k_p, v_p, logf_p, ck_p, qb, kb, vb, olru_p, omem_p, xtail_p, htail_p) = _proj_prompt(
        x_prompt, mk_p, mv_p, w)
    ofox_p = _fox_prompt(qb, kb, vb, ck_p)
    y_p, gtail_p = _merge_ffn_prompt(x_prompt, ofox_p, olru_p, omem_p, w)

    x2 = x_sample.reshape(DB * T, D)
    cstate_tm = jnp.swapaxes(state_lru_conv[l], 0, 1)
    fstate_tm = jnp.swapaxes(state_ffn_conv[l], 0, 1)
    (q_s, k_s, v_s, logf_s, qm_s, olru_s, xtail_s, hlast_s) = _proj_sample(
        x2, cstate_tm, state_lru_h[l], w)
    omem_s = _mem_sample(qm_s, cache_mem_k[l], cache_mem_v[l])
    cache_kt = jnp.transpose(cache_k[l], (0, 2, 3, 1)).reshape(n_pool, FOX_W, PAGE)
    cache_vt = jnp.transpose(cache_v[l], (0, 2, 3, 1)).reshape(n_pool, FOX_W, PAGE)
    cache_ft = jnp.transpose(cache_logf[l], (0, 2, 1))
    ofox_s = _fox_sample(page_table, q_s, k_s, v_s, logf_s.reshape(DB, T, N_HEADS),
                         cache_kt, cache_vt, cache_ft)
    y_s, gtail_s = _merge_ffn_sample(x2, ofox_s, olru_s, omem_s, fstate_tm, w)

    heads = lambda a, n: a.reshape(1, n, -1, N_HEADS, HEAD_DIM)
    mem_heads = lambda a: a.reshape(1, B, -1, MEM_HEADS, MEM_DIM)
    return (
        y_p,
        y_s.reshape(DB, T, D),
        heads(k_p, B), heads(v_p, B), logf_p[None],
        mem_heads(mk_p), mem_heads(mv_p),
        htail_p[None, :, -1, :],
        xtail_p[None, :, SUBLANES - (CONV_LRU - 1):, :],
        gtail_p[None, :, SUBLANES - (CONV_FFN - 1):, :],
        heads(k_s, DB), heads(v_s, DB), logf_s.reshape(1, DB, T, N_HEADS),
        hlast_s[None],
        jnp.swapaxes(xtail_s, 0, 1)[None],
        jnp.swapaxes(gtail_s, 0, 1)[None],
    )
```

```python
import functools

import jax
import jax.numpy as jnp
from jax import lax
from jax.experimental import pallas as pl
from jax.experimental.pallas import tpu as pltpu

F32 = jnp.float32
BF16 = jnp.bfloat16

EPS = 1e-6
LRU_C = 8.0
NEG_BIG = -1e30
LANES = 128
SUBLANES = 8
VMEM_LIMIT = 56 * 1024 * 1024

N_HEADS = 8
HEAD_DIM = 64
FOX_W = N_HEADS * HEAD_DIM
LRU_W = 512
MEM_HEADS = 4
MEM_DIM = 128
MEM_W = MEM_HEADS * MEM_DIM
CONV_LRU = 4
CONV_FFN = 3
PAGE = 128

TM_PROJ = 256
TM_FFN = 256
TQ = 512
FF_CHUNK = 1024
PAGES_PER_STEP = 16
FETCH_SLOTS = 3
MEM_BATCH = 8
SAMPLE_ROWS = 256


def _rms(x, g):
    return x * lax.rsqrt(jnp.mean(x * x, axis=-1, keepdims=True) + EPS) * g


def _log_sigmoid(x):
    return jnp.minimum(x, 0.0) - jnp.log1p(jnp.exp(-jnp.abs(x)))


def _gelu(x):
    return 0.5 * x * (1.0 + jnp.tanh(0.7978845608028654 * (x + 0.044715 * (x * x * x))))


def _dot(a, b):
    return jnp.dot(a, b, preferred_element_type=F32)


def _dot_nt(a, b):
    return lax.dot_general(a, b, (((1,), (1,)), ((), ())), preferred_element_type=F32)


def _lru_coeffs(xc, wax_ref, bax_ref, lam_ref):
    gates = _dot(xc.astype(BF16), wax_ref[...]) + bax_ref[...]
    r = jax.nn.sigmoid(gates[:, :LRU_W])
    i = jax.nn.sigmoid(gates[:, LRU_W:])
    log_a = LRU_C * r * _log_sigmoid(lam_ref[...])
    a = jnp.exp(log_a)
    th = jnp.tanh(log_a)
    gated = jnp.sqrt(-2.0 * th / (1.0 - th)) * i * xc
    return a, gated


def _mem_attend_head(q, mk, mv):
    s = _dot_nt(q, mk)
    m = jnp.max(s, axis=-1, keepdims=True)
    p = jnp.exp(s - m)
    l = jnp.sum(p, axis=-1, keepdims=True)
    return _dot(p.astype(BF16), mv) / l


def _lane_cumsum(x, width):
    lane = lax.broadcasted_iota(jnp.int32, x.shape, x.ndim - 1)
    d = 1
    while d < width:
        x = x + jnp.where(lane >= d, pltpu.roll(x, d, axis=x.ndim - 1), 0.0)
        d *= 2
    return x


def _park(scr, x):
    for c in range(scr.shape[0]):
        scr[c] = x[:, c * LANES:(c + 1) * LANES]


def _unpark(scr):
    return jnp.concatenate([scr[c] for c in range(scr.shape[0])], axis=1)


def _token_rows(scr, t, nseq, ntok):
    return jnp.concatenate([scr[c, pl.ds(t, nseq, stride=ntok), :] for c in range(scr.shape[0])],
                           axis=1)


def _set_token_rows(scr, t, nseq, ntok, x):
    for c in range(scr.shape[0]):
        scr[c, pl.ds(t, nseq, stride=ntok), :] = x[:, c * LANES:(c + 1) * LANES]


def _const_spec(shape):
    nd = len(shape)
    return pl.BlockSpec(shape, lambda *_: (0,) * nd, pipeline_mode=pl.Buffered(1))


def _mem_kv_kernel(mem_ref, g_ref, w_ref, mk_ref, mv_ref):
    u = _rms(mem_ref[0], g_ref[...]).astype(BF16)
    z = _dot(u, w_ref[...])
    mk_ref[0] = z[:, :MEM_W]
    mv_ref[0] = z[:, MEM_W:]


def _mem_kv(mem, g_mem, w_mem_kv):
    B, M, D = mem.shape
    return pl.pallas_call(
        _mem_kv_kernel,
        grid=(B,),
        in_specs=[pl.BlockSpec((1, M, D), lambda b: (b, 0, 0)),
                  _const_spec((1, D)), _const_spec((D, 2 * MEM_W))],
        out_specs=[pl.BlockSpec((1, M, MEM_W), lambda b: (b, 0, 0))] * 2,
        out_shape=[jax.ShapeDtypeStruct((B, M, MEM_W), F32)] * 2,
        compiler_params=pltpu.CompilerParams(dimension_semantics=("arbitrary",),
                                             vmem_limit_bytes=VMEM_LIMIT),
        name="mem_kv",
    )(mem, g_mem, w_mem_kv)


def _proj_prompt_kernel(x_ref, gmix_ref, wqkv_ref, wf_ref, bf_ref, wrnn_ref, wqm_ref,
                        wconv_ref, bconv_ref, wax_ref, bax_ref, lam_ref, mk_ref, mv_ref,
                        k_ref, v_ref, logf_ref, ck_ref, qb_ref, kb_ref, vb_ref,
                        olru_ref, omem_ref, xtail_ref, htail_ref,
                        xext_scr, a_scr, g_scr, h_scr, c_scr):
    tm = x_ref.shape[1]
    pad = a_scr.shape[0] - tm
    t = pl.program_id(1)

    @pl.when(t == 0)
    def _():
        xext_scr[0:SUBLANES, :] = jnp.zeros((SUBLANES, LRU_W), F32)
        h_scr[...] = jnp.zeros_like(h_scr)
        c_scr[...] = jnp.zeros_like(c_scr)
        a_scr[0:pad, :] = jnp.ones((pad, LRU_W), F32)
        g_scr[0:pad, :] = jnp.zeros((pad, LRU_W), F32)

    u = _rms(x_ref[0], gmix_ref[...]).astype(BF16)

    zqkv = _dot(u, wqkv_ref[...])
    k_ref[0] = zqkv[:, FOX_W:2 * FOX_W]
    v_ref[0] = zqkv[:, 2 * FOX_W:]
    qb_ref[0] = zqkv[:, :FOX_W].astype(BF16)
    kb_ref[0] = zqkv[:, FOX_W:2 * FOX_W].astype(BF16)
    vb_ref[0] = zqkv[:, 2 * FOX_W:].astype(BF16)

    logf = _log_sigmoid(_dot(u, wf_ref[...]) + bf_ref[...])
    logf_ref[0] = logf[:, :N_HEADS]
    c = _lane_cumsum(logf.T[:N_HEADS, :], tm) + c_scr[:, 0:1]
    ck_ref[0] = c
    c_scr[...] = jnp.broadcast_to(c[:, tm - 1:tm], c_scr.shape)

    zr = _dot(u, wrnn_ref[...])
    xr = zr[:, :LRU_W]
    xext_scr[SUBLANES:SUBLANES + tm, :] = xr
    xtail_ref[0] = xr[tm - SUBLANES:, :]
    xc = bconv_ref[...] + wconv_ref[CONV_LRU - 1:CONV_LRU, :] * xr
    for j in range(CONV_LRU - 1):
        off = SUBLANES - (CONV_LRU - 1) + j
        xc = xc + wconv_ref[j:j + 1, :] * xext_scr[off:off + tm, :]
    xext_scr[0:SUBLANES, :] = xr[tm - SUBLANES:, :]

    a, gated = _lru_coeffs(xc, wax_ref, bax_ref, lam_ref)
    d = 1
    while d < tm:
        a_scr[pad:pad + tm, :] = a
        g_scr[pad:pad + tm, :] = gated
        gated = gated + a * g_scr[pad - d:pad - d + tm, :]
        a = a * a_scr[pad - d:pad - d + tm, :]
        d *= 2
    hs = gated + a * h_scr[0:1, :]
    h_scr[...] = jnp.broadcast_to(hs[tm - 1:tm, :], h_scr.shape)
    htail_ref[0] = hs[tm - SUBLANES:, :]
    olru_ref[0] = (hs * _gelu(zr[:, LRU_W:])).astype(BF16)

    zq = _dot(u, wqm_ref[...]) * (MEM_DIM ** -0.5)
    for h in range(MEM_HEADS):
        sl = slice(h * MEM_DIM, (h + 1) * MEM_DIM)
        o = _mem_attend_head(zq[:, sl].astype(BF16), mk_ref[0, :, sl].astype(BF16),
                             mv_ref[0, :, sl].astype(BF16))
        omem_ref[0, :, sl] = o.astype(BF16)


def _proj_prompt(x, mk, mv, w):
    B, S, D = x.shape
    M = mk.shape[1]
    tm = TM_PROJ
    nt = S // tm
    row = lambda width: pl.BlockSpec((1, tm, width), lambda b, t: (b, t, 0))
    per_b = lambda rows, width: pl.BlockSpec((1, rows, width), lambda b, t: (b, 0, 0))
    consts = [w["g_mix"], w["w_qkv"], w["w_f"], w["b_f"], w["w_rnn"], w["w_qm"],
              w["w_lru_conv"], w["b_lru_conv"], w["w_ax"], w["b_ax"], w["lam"]]
    out_shape = [
        jax.ShapeDtypeStruct((B, S, FOX_W), F32),
        jax.ShapeDtypeStruct((B, S, FOX_W), F32),
        jax.ShapeDtypeStruct((B, S, N_HEADS), F32),
        jax.ShapeDtypeStruct((B, N_HEADS, S), F32),
        jax.ShapeDtypeStruct((B, S, FOX_W), BF16),
        jax.ShapeDtypeStruct((B, S, FOX_W), BF16),
        jax.ShapeDtypeStruct((B, S, FOX_W), BF16),
        jax.ShapeDtypeStruct((B, S, LRU_W), BF16),
        jax.ShapeDtypeStruct((B, S, MEM_W), BF16),
        jax.ShapeDtypeStruct((B, SUBLANES, LRU_W), F32),
        jax.ShapeDtypeStruct((B, SUBLANES, LRU_W), F32),
    ]
    out_specs = [row(FOX_W), row(FOX_W), row(N_HEADS),
                 pl.BlockSpec((1, N_HEADS, tm), lambda b, t: (b, 0, t)),
                 row(FOX_W), row(FOX_W), row(FOX_W), row(LRU_W), row(MEM_W),
                 per_b(SUBLANES, LRU_W), per_b(SUBLANES, LRU_W)]
    pad = tm // 2
    return pl.pallas_call(
        _proj_prompt_kernel,
        grid=(B, nt),
        in_specs=[row(D)] + [_const_spec(c.shape) for c in consts] + [per_b(M, MEM_W)] * 2,
        out_specs=out_specs,
        out_shape=out_shape,
        scratch_shapes=[pltpu.VMEM((tm + SUBLANES, LRU_W), F32),
                        pltpu.VMEM((tm + pad, LRU_W), F32),
                        pltpu.VMEM((tm + pad, LRU_W), F32),
                        pltpu.VMEM((SUBLANES, LRU_W), F32),
                        pltpu.VMEM((N_HEADS, LANES), F32)],
        compiler_params=pltpu.CompilerParams(dimension_semantics=("arbitrary", "arbitrary"),
                                             vmem_limit_bytes=VMEM_LIMIT),
        name="proj_prompt",
    )(x, *consts, mk, mv)


def _fox_prompt_kernel(q_ref, k_ref, v_ref, ck_ref, o_ref, s_scr, p_scr, m_scr, acc_scr):
    tq = q_ref.shape[1]
    i = pl.program_id(2)
    q2 = q_ref[0]
    first = lax.broadcasted_iota(jnp.int32, q2.shape, 1) < HEAD_DIM
    zero = jnp.zeros_like(q2)
    qs = jnp.concatenate([jnp.where(first, q2, zero), jnp.where(first, zero, q2)], axis=0)
    ones = jnp.ones((tq, LANES), BF16)

    def logits(j):
        return _dot_nt(qs, k_ref[0, pl.ds(pl.multiple_of(j * tq, tq), tq), :])

    def weighted_values(p, j):
        v_aug = jnp.concatenate([v_ref[0, pl.ds(pl.multiple_of(j * tq, tq), tq), :], ones], axis=1)
        return _dot(p, v_aug)

    def softmax_block(j, slot, pv, masked):
        start = pl.multiple_of(j * tq, tq)
        if masked:
            causal = (lax.broadcasted_iota(jnp.int32, (tq, tq), 1)
                      <= lax.broadcasted_iota(jnp.int32, (tq, tq), 0))
        for hh in range(2):
            rows = slice(hh * tq, (hh + 1) * tq)
            s = s_scr[slot, rows, :] - ck_ref[0, 0, hh:hh + 1, pl.ds(start, tq)]
            if masked:
                s = jnp.where(causal, s, NEG_BIG)
            m_old = m_scr[rows, :]
            m_new = jnp.maximum(m_old, jnp.max(s, axis=-1, keepdims=True))
            p_scr[slot, rows, :] = jnp.exp(s - m_new).astype(BF16)
            acc_scr[rows, :] = jnp.exp(m_old - m_new) * (acc_scr[rows, :] + pv[rows, :])
            m_scr[rows, :] = m_new

    def step(t, slot):
        other = 1 - slot
        pv = weighted_values(p_scr[other], jnp.maximum(t - 1, 0))
        s_scr[other] = logits(t + 1)
        softmax_block(t, slot, pv, False)

    def finish(slot):
        pv = weighted_values(p_scr[1 - slot], jnp.maximum(i - 1, 0))
        softmax_block(i, slot, pv, True)
        acc = acc_scr[...] + weighted_values(p_scr[slot], i)
        out = acc[:, :LANES] / acc[:, LANES:]
        o_ref[0] = jnp.where(first, out[:tq, :], out[tq:, :]).astype(o_ref.dtype)

    m_scr[...] = jnp.full_like(m_scr, NEG_BIG)
    acc_scr[...] = jnp.zeros_like(acc_scr)
    p_scr[1] = jnp.zeros(p_scr.shape[1:], BF16)
    s_scr[0] = logits(0)

    def two_steps(tt, carry):
        step(2 * tt, 0)
        step(2 * tt + 1, 1)
        return carry

    lax.fori_loop(0, i // 2, two_steps, 0)

    @pl.when(i % 2 == 0)
    def _():
        finish(0)

    @pl.when(i % 2 == 1)
    def _():
        step(i - 1, 0)
        finish(1)


def _fox_prompt(qb, kb, vb, ck):
    B, S, W = qb.shape
    npair = W // LANES
    ck4 = ck.reshape(B, npair, 2, S)
    return pl.pallas_call(
        _fox_prompt_kernel,
        grid=(B, npair, S // TQ),
        in_specs=[pl.BlockSpec((1, TQ, LANES), lambda b, h, i: (b, i, h)),
                  pl.BlockSpec((1, S, LANES), lambda b, h, i: (b, 0, h)),
                  pl.BlockSpec((1, S, LANES), lambda b, h, i: (b, 0, h)),
                  pl.BlockSpec((1, 1, 2, S), lambda b, h, i: (b, h, 0, 0))],
        out_specs=pl.BlockSpec((1, TQ, LANES), lambda b, h, i: (b, i, h)),
        out_shape=jax.ShapeDtypeStruct((B, S, W), BF16),
        scratch_shapes=[pltpu.VMEM((2, 2 * TQ, TQ), F32), pltpu.VMEM((2, 2 * TQ, TQ), BF16),
                        pltpu.VMEM((2 * TQ, 1), F32), pltpu.VMEM((2 * TQ, 2 * LANES), F32)],
        compiler_params=pltpu.CompilerParams(
            dimension_semantics=("arbitrary", "arbitrary", "arbitrary"),
            vmem_limit_bytes=VMEM_LIMIT),
        name="fox_prompt",
    )(qb, kb, vb, ck4)


def _merge(x, branches, gmix_ref, wgate_ref, bgate_ref, wout_ref):
    D = x.shape[1]
    u = _rms(x, gmix_ref[...]).astype(BF16)
    merged = None
    for j, (o, wo_ref) in enumerate(branches):
        sl = slice(j * D, (j + 1) * D)
        gate = jax.nn.sigmoid(_dot(u, wgate_ref[:, sl]) + bgate_ref[:, sl])
        term = gate * _dot(o, wo_ref[...])
        merged = term if merged is None else merged + term
    return x + _dot(merged.astype(BF16), wout_ref[...])


def _merge_ffn_prompt_kernel(x_ref, ofox_ref, olru_ref, omem_ref, gmix_ref, wgate_ref, bgate_ref,
                             wofox_ref, wolru_ref, womem_ref, wout_ref, gffn_ref, wfg_ref, wfu_ref,
                             wfc_ref, bfc_ref, wfd_ref, gfin_ref,
                             y_ref, gtail_ref, gext_scr):
    tm = x_ref.shape[1]
    dff = wfg_ref.shape[1]
    t = pl.program_id(1)

    @pl.when(t == 0)
    def _():
        gext_scr[0:SUBLANES, :] = jnp.zeros((SUBLANES, dff), F32)

    h = _merge(x_ref[0], [(ofox_ref[0], wofox_ref), (olru_ref[0], wolru_ref),
                          (omem_ref[0], womem_ref)], gmix_ref, wgate_ref, bgate_ref, wout_ref)
    v2 = _rms(h, gffn_ref[...]).astype(BF16)
    y = h
    for c in range(dff // FF_CHUNK):
        sl = slice(c * FF_CHUNK, (c + 1) * FF_CHUNK)
        g = _dot(v2, wfg_ref[:, sl])
        gext_scr[SUBLANES:SUBLANES + tm, sl] = g
        gc = bfc_ref[:, sl] + wfc_ref[CONV_FFN - 1:CONV_FFN, sl] * g
        for j in range(CONV_FFN - 1):
            off = SUBLANES - (CONV_FFN - 1) + j
            gc = gc + wfc_ref[j:j + 1, sl] * gext_scr[off:off + tm, sl]
        act = (_gelu(gc) * _dot(v2, wfu_ref[:, sl])).astype(BF16)
        y = y + _dot(act, wfd_ref[sl, :])
    tail = gext_scr[tm:tm + SUBLANES, :]
    gtail_ref[0] = tail
    gext_scr[0:SUBLANES, :] = tail
    y_ref[0] = _rms(y, gfin_ref[...])


def _merge_ffn_consts(w):
    return [w["g_mix"], w["w_gate"], w["b_gate"], w["w_o_fox"], w["w_o_lru"], w["w_o_mem"],
            w["w_out"], w["g_ffn"], w["w_ffn_gate"], w["w_ffn_up"], w["w_ffn_conv"],
            w["b_ffn_conv"], w["w_ffn_down"], w["g_final"]]


def _merge_ffn_prompt(x, ofox, olru, omem, w):
    B, S, D = x.shape
    dff = w["w_ffn_gate"].shape[1]
    tm = TM_FFN
    row = lambda width: pl.BlockSpec((1, tm, width), lambda b, t: (b, t, 0))
    consts = _merge_ffn_consts(w)
    return pl.pallas_call(
        _merge_ffn_prompt_kernel,
        grid=(B, S // tm),
        in_specs=[row(D), row(FOX_W), row(LRU_W), row(MEM_W)]
                 + [_const_spec(c.shape) for c in consts],
        out_specs=[row(D), pl.BlockSpec((1, SUBLANES, dff), lambda b, t: (b, 0, 0))],
        out_shape=[jax.ShapeDtypeStruct((B, S, D), F32),
                   jax.ShapeDtypeStruct((B, SUBLANES, dff), F32)],
        scratch_shapes=[pltpu.VMEM((tm + SUBLANES, dff), F32)],
        compiler_params=pltpu.CompilerParams(dimension_semantics=("arbitrary", "arbitrary"),
                                             vmem_limit_bytes=VMEM_LIMIT),
        name="merge_ffn_prompt",
    )(x, ofox, olru, omem, *consts)


def _proj_sample_kernel(x_ref, gmix_ref, wqkv_ref, wf_ref, bf_ref, wrnn_ref, wqm_ref,
                        wconv_ref, bconv_ref, wax_ref, bax_ref, lam_ref, cstate_ref, h0_ref,
                        q_ref, k_ref, v_ref, logf_ref, qm_ref, olru_ref, xtail_ref, hlast_ref,
                        x_scr, h_scr):
    nb = h0_ref.shape[0]
    nt = x_ref.shape[0] // nb
    u = _rms(x_ref[...], gmix_ref[...]).astype(BF16)
    zqkv = _dot(u, wqkv_ref[...])
    q_ref[...] = zqkv[:, :FOX_W]
    k_ref[...] = zqkv[:, FOX_W:2 * FOX_W]
    v_ref[...] = zqkv[:, 2 * FOX_W:]
    logf = _log_sigmoid(_dot(u, wf_ref[...]) + bf_ref[...])
    logf_ref[...] = logf[:, :N_HEADS]
    qm_ref[...] = _dot(u, wqm_ref[...]) * (MEM_DIM ** -0.5)

    zr = _dot(u, wrnn_ref[...])
    _park(x_scr, zr[:, :LRU_W])
    xs = [cstate_ref[j] for j in range(CONV_LRU - 1)]
    xs += [_token_rows(x_scr, t, nb, nt) for t in range(nt)]
    h = h0_ref[...]
    for t in range(nt):
        xc = bconv_ref[...]
        for j in range(CONV_LRU):
            xc = xc + wconv_ref[j:j + 1, :] * xs[t + j]
        a, gated = _lru_coeffs(xc, wax_ref, bax_ref, lam_ref)
        h = a * h + gated
        _set_token_rows(h_scr, t, nb, nt, h)
    hlast_ref[...] = h
    for j in range(CONV_LRU - 1):
        xtail_ref[j] = xs[nt + j]
    olru_ref[...] = (_unpark(h_scr) * _gelu(zr[:, LRU_W:])).astype(BF16)


def _proj_sample(x2, cstate_tm, h0, w):
    R, D = x2.shape
    nb = h0.shape[0]
    nt = R // nb
    rb = SAMPLE_ROWS
    sb = rb // nt
    consts = [w["g_mix"], w["w_qkv"], w["w_f"], w["b_f"], w["w_rnn"], w["w_qm"],
              w["w_lru_conv"], w["b_lru_conv"], w["w_ax"], w["b_ax"], w["lam"]]
    row = lambda width: pl.BlockSpec((rb, width), lambda i: (i, 0))
    state = pl.BlockSpec((CONV_LRU - 1, sb, LRU_W), lambda i: (0, i, 0))
    seq = pl.BlockSpec((sb, LRU_W), lambda i: (i, 0))
    out_shape = [
        jax.ShapeDtypeStruct((R, FOX_W), F32),
        jax.ShapeDtypeStruct((R, FOX_W), F32),
        jax.ShapeDtypeStruct((R, FOX_W), F32),
        jax.ShapeDtypeStruct((R, N_HEADS), F32),
        jax.ShapeDtypeStruct((R, MEM_W), F32),
        jax.ShapeDtypeStruct((R, LRU_W), BF16),
        jax.ShapeDtypeStruct((CONV_LRU - 1, nb, LRU_W), F32),
        jax.ShapeDtypeStruct((nb, LRU_W), F32),
    ]
    return pl.pallas_call(
        _proj_sample_kernel,
        grid=(R // rb,),
        in_specs=[row(D)] + [_const_spec(c.shape) for c in consts] + [state, seq],
        out_specs=[row(FOX_W), row(FOX_W), row(FOX_W), row(N_HEADS), row(MEM_W), row(LRU_W),
                   state, seq],
        out_shape=out_shape,
        scratch_shapes=[pltpu.VMEM((LRU_W // LANES, rb, LANES), F32)] * 2,
        compiler_params=pltpu.CompilerParams(dimension_semantics=("arbitrary",),
                                             vmem_limit_bytes=VMEM_LIMIT),
        name="proj_sample",
    )(x2, *consts, cstate_tm, h0)


def _mem_sample_kernel(q_ref, mk_ref, mv_ref, o_ref):
    nb = mk_ref.shape[0]
    nt = q_ref.shape[0] // nb
    for b in range(nb):
        rows = slice(b * nt, (b + 1) * nt)
        for h in range(MEM_HEADS):
            sl = slice(h * MEM_DIM, (h + 1) * MEM_DIM)
            o_ref[rows, sl] = _mem_attend_head(q_ref[rows, sl].astype(BF16),
                                               mk_ref[b, :, h, :].astype(BF16),
                                               mv_ref[b, :, h, :].astype(BF16))


def _mem_sample(qm, mk, mv):
    R, W = qm.shape
    DB, M, H, Dm = mk.shape
    nt = R // DB
    nb = MEM_BATCH
    kv_spec = pl.BlockSpec((nb, M, H, Dm), lambda i: (i, 0, 0, 0))
    return pl.pallas_call(
        _mem_sample_kernel,
        grid=(DB // nb,),
        in_specs=[pl.BlockSpec((nb * nt, W), lambda i: (i, 0)), kv_spec, kv_spec],
        out_specs=pl.BlockSpec((nb * nt, W), lambda i: (i, 0)),
        out_shape=jax.ShapeDtypeStruct((R, W), F32),
        compiler_params=pltpu.CompilerParams(dimension_semantics=("arbitrary",),
                                             vmem_limit_bytes=VMEM_LIMIT),
        name="mem_sample",
    )(qm, mk, mv)


def _fox_sample_kernel(pt_ref, q_ref, kn_ref, vn_ref, lfn_ref, k_hbm, v_hbm, f_hbm, o_ref,
                       k_buf, v_buf, f_buf, k_sem, v_sem, f_sem,
                       kb_scr, vb_scr, qbd_scr, m_scr, l_scr, acc_scr, c_scr, pack_scr):
    npg = PAGES_PER_STEP
    nslot = k_buf.shape[0]
    nt = q_ref.shape[0]
    nrow = N_HEADS * nt
    c = pl.program_id(1)
    nch = pl.num_programs(1)
    step = pl.program_id(0) * nch + c
    last = pl.num_programs(0) * nch - 1
    lane = lax.broadcasted_iota(jnp.int32, (nt, FOX_W), 1)
    streams = ((k_hbm, k_buf, k_sem), (v_hbm, v_buf, v_sem), (f_hbm, f_buf, f_sem))

    def page_copies(slot, pages):
        return [pltpu.make_async_copy(hbm.at[pages[p]], buf.at[slot, p], sem.at[slot])
                for hbm, buf, sem in streams for p in range(npg)]

    def fetch(s):
        src = jnp.minimum(s, last)
        row = src // nch
        col = (src % nch) * npg
        for cp in page_copies(s % nslot, [pt_ref[row, col + p] for p in range(npg)]):
            cp.start()

    def wait(slot):
        for cp in page_copies(slot, [0] * npg):
            cp.wait()

    @pl.when(step == 0)
    def _():
        for s in range(nslot - 1):
            fetch(jnp.int32(s))

    slot = step % nslot
    wait(slot)
    fetch(step + (nslot - 1))
    k_pages = [k_buf.at[slot, p] for p in range(npg)]
    v_pages = [v_buf.at[slot, p] for p in range(npg)]
    f_pages = [f_buf.at[slot, p] for p in range(npg)]

    @pl.when(c == 0)
    def _():
        q = q_ref[...]
        for h in range(N_HEADS):
            mine = (lane >= h * HEAD_DIM) & (lane < (h + 1) * HEAD_DIM)
            qbd_scr[h * nt:(h + 1) * nt, :] = jnp.where(mine, q, 0.0).astype(BF16)
        m_scr[...] = jnp.full_like(m_scr, NEG_BIG)
        l_scr[...] = jnp.zeros_like(l_scr)
        acc_scr[...] = jnp.zeros_like(acc_scr)
        c_scr[...] = jnp.zeros_like(c_scr)

    def update(s, pv):
        m = m_scr[...]
        m_new = jnp.maximum(m, jnp.max(s, axis=-1, keepdims=True))
        alpha = jnp.exp(m - m_new)
        p = jnp.exp(s - m_new)
        l_scr[...] = alpha * l_scr[...] + jnp.sum(p, axis=-1, keepdims=True)
        acc_scr[...] = alpha * acc_scr[...] + pv(p.astype(BF16))
        m_scr[...] = m_new

    def head_rows(cb):
        return jnp.concatenate(
            [jnp.broadcast_to(cb[h:h + 1, :], (nt, cb.shape[1])) for h in range(N_HEADS)], axis=0)

    z = _lane_cumsum(jnp.concatenate([f_pages[p][...] for p in range(npg)], axis=0), PAGE)
    tiles = [z[p * N_HEADS:(p + 1) * N_HEADS, :] for p in range(npg)]
    totals = [jnp.broadcast_to(t[:, PAGE - 1:PAGE], t.shape) for t in tiles]
    off = c_scr[...]
    for p in range(npg):
        tiles[p] = tiles[p] + off
        off = off + totals[p]
    c_scr[...] = off

    for p in range(npg):
        kb_scr[:, p * PAGE:(p + 1) * PAGE] = k_pages[p][...].astype(BF16)
        vb_scr[:, p * PAGE:(p + 1) * PAGE] = v_pages[p][...].astype(BF16)
    s = _dot(qbd_scr[...], kb_scr[...]) - head_rows(jnp.concatenate(tiles, axis=1))
    update(s, lambda p: _dot_nt(p, vb_scr[...]))

    @pl.when(c == pl.num_programs(1) - 1)
    def _():
        zeros = jnp.zeros((PAGE - nt, FOX_W), F32)
        kn = jnp.concatenate([kn_ref[...], zeros], axis=0).astype(BF16)
        vn = jnp.concatenate([vn_ref[...], zeros], axis=0).astype(BF16)
        pack_scr[...] = jnp.zeros_like(pack_scr)
        pack_scr[0:nt, 0:N_HEADS] = lfn_ref[0]
        cn = _lane_cumsum(pack_scr[...].T[:N_HEADS, :], PAGE) + c_scr[...]
        s = _dot_nt(qbd_scr[...], kn) - head_rows(cn)
        tok = lax.broadcasted_iota(jnp.int32, (nrow, PAGE), 0) % nt
        key = lax.broadcasted_iota(jnp.int32, (nrow, PAGE), 1)
        update(jnp.where(key <= tok, s, NEG_BIG), lambda p: _dot(p, vn))
        acc = acc_scr[...] / l_scr[...]
        out = acc[0:nt, :]
        for h in range(1, N_HEADS):
            out = jnp.where(lane >= h * HEAD_DIM, acc[h * nt:(h + 1) * nt, :], out)
        o_ref[...] = out

    @pl.when(step == last)
    def _():
        for ahead in range(1, nslot):
            wait((step + ahead) % nslot)


def _fox_sample(page_table, q, k_new, v_new, logf_new, cache_kt, cache_vt, cache_ft):
    DB, npages = page_table.shape
    R, W = q.shape
    nt = R // DB
    npg = PAGES_PER_STEP
    nk = npg * PAGE

    nslot = FETCH_SLOTS
    tok = pl.BlockSpec((nt, W), lambda b, c, pt: (b, 0))
    hbm = pl.BlockSpec(memory_space=pl.ANY)
    grid_spec = pltpu.PrefetchScalarGridSpec(
        num_scalar_prefetch=1,
        grid=(DB, npages // npg),
        in_specs=[tok, tok, tok, pl.BlockSpec((1, nt, N_HEADS), lambda b, c, pt: (b, 0, 0)),
                  hbm, hbm, hbm],
        out_specs=tok,
        scratch_shapes=[pltpu.VMEM((nslot, npg, W, PAGE), F32),
                        pltpu.VMEM((nslot, npg, W, PAGE), F32),
                        pltpu.VMEM((nslot, npg, N_HEADS, PAGE), F32),
                        pltpu.SemaphoreType.DMA((nslot,)), pltpu.SemaphoreType.DMA((nslot,)),
                        pltpu.SemaphoreType.DMA((nslot,)),
                        pltpu.VMEM((W, nk), BF16), pltpu.VMEM((W, nk), BF16),
                        pltpu.VMEM((N_HEADS * nt, W), BF16),
                        pltpu.VMEM((N_HEADS * nt, 1), F32), pltpu.VMEM((N_HEADS * nt, 1), F32),
                        pltpu.VMEM((N_HEADS * nt, W), F32),
                        pltpu.VMEM((N_HEADS, PAGE), F32), pltpu.VMEM((PAGE, LANES), F32)],
    )
    return pl.pallas_call(
        _fox_sample_kernel,
        grid_spec=grid_spec,
        out_shape=jax.ShapeDtypeStruct((R, W), F32),
        compiler_params=pltpu.CompilerParams(dimension_semantics=("arbitrary", "arbitrary"),
                                             vmem_limit_bytes=VMEM_LIMIT),
        name="fox_sample",
    )(page_table, q, k_new, v_new, logf_new, cache_kt, cache_vt, cache_ft)


def _merge_ffn_sample_kernel(x_ref, ofox_ref, olru_ref, omem_ref, fstate_ref, gmix_ref, wgate_ref,
                             bgate_ref, wofox_ref, wolru_ref, womem_ref, wout_ref, gffn_ref,
                             wfg_ref, wfu_ref, wfc_ref, bfc_ref, wfd_ref, gfin_ref,
                             y_ref, gtail_ref, g_scr, gc_scr):
    nb = fstate_ref.shape[1]
    nt = x_ref.shape[0] // nb
    dff = wfg_ref.shape[1]
    h = _merge(x_ref[...], [(ofox_ref[...].astype(BF16), wofox_ref), (olru_ref[...], wolru_ref),
                            (omem_ref[...].astype(BF16), womem_ref)],
               gmix_ref, wgate_ref, bgate_ref, wout_ref)
    v2 = _rms(h, gffn_ref[...]).astype(BF16)
    y = h
    for c in range(dff // FF_CHUNK):
        sl = slice(c * FF_CHUNK, (c + 1) * FF_CHUNK)
        _park(g_scr, _dot(v2, wfg_ref[:, sl]))
        gs = [fstate_ref[j, :, sl] for j in range(CONV_FFN - 1)]
        gs += [_token_rows(g_scr, t, nb, nt) for t in range(nt)]
        for t in range(nt):
            gc = bfc_ref[:, sl]
            for j in range(CONV_FFN):
                gc = gc + wfc_ref[j:j + 1, sl] * gs[t + j]
            _set_token_rows(gc_scr, t, nb, nt, gc)
        for j in range(CONV_FFN - 1):
            gtail_ref[j, :, sl] = gs[nt + j]
        act = (_gelu(_unpark(gc_scr)) * _dot(v2, wfu_ref[:, sl])).astype(BF16)
        y = y + _dot(act, wfd_ref[sl, :])
    y_ref[...] = _rms(y, gfin_ref[...])


def _merge_ffn_sample(x2, ofox, olru, omem, fstate_tm, w):
    R, D = x2.shape
    nb = fstate_tm.shape[1]
    nt = R // nb
    dff = w["w_ffn_gate"].shape[1]
    rb = SAMPLE_ROWS
    sb = rb // nt
    consts = _merge_ffn_consts(w)
    row = lambda width: pl.BlockSpec((rb, width), lambda i: (i, 0))
    state = pl.BlockSpec((CONV_FFN - 1, sb, dff), lambda i: (0, i, 0))
    return pl.pallas_call(
        _merge_ffn_sample_kernel,
        grid=(R // rb,),
        in_specs=[row(D), row(FOX_W), row(LRU_W), row(MEM_W), state]
                 + [_const_spec(c.shape) for c in consts],
        out_specs=[row(D), state],
        out_shape=[jax.ShapeDtypeStruct((R, D), F32),
                   jax.ShapeDtypeStruct((CONV_FFN - 1, nb, dff), F32)],
        scratch_shapes=[pltpu.VMEM((FF_CHUNK // LANES, rb, LANES), F32)] * 2,
        compiler_params=pltpu.CompilerParams(dimension_semantics=("arbitrary",),
                                             vmem_limit_bytes=VMEM_LIMIT),
        name="merge_ffn_sample",
    )(x2, ofox, olru, omem, fstate_tm, *consts)


def _block_diag(wb):
    nb, bs, _ = wb.shape
    eye = jnp.eye(nb, dtype=wb.dtype)
    return (eye[:, None, :, None] * wb[:, :, None, :]).reshape(nb * bs, nb * bs)


def _prep_weights(l, g_mix, w_in, b_f, w_o_fox, w_lru_conv, b_lru_conv, w_lru_a, b_lru_a, w_lru_x,
                  b_lru_x, lru_lambda, w_o_lru, w_o_mem, w_gate, b_gate, w_out, g_ffn, w_ffn_gate,
                  w_ffn_up, w_ffn_conv, b_ffn_conv, w_ffn_down, g_final):
    wi = w_in[l]
    o_f = 3 * FOX_W
    o_r = o_f + N_HEADS
    o_m = o_r + 2 * LRU_W
    row = lambda v: v.reshape(1, -1)
    w_qkv = jnp.concatenate([wi[:, :FOX_W] * (HEAD_DIM ** -0.5), wi[:, FOX_W:o_f]], axis=1)
    return {
        "g_mix": row(g_mix[l]),
        "w_qkv": w_qkv.astype(BF16),
        "w_f": jnp.pad(wi[:, o_f:o_r], ((0, 0), (0, LANES - N_HEADS))).astype(BF16),
        "b_f": jnp.pad(row(b_f[l]), ((0, 0), (0, LANES - N_HEADS))),
        "w_rnn": wi[:, o_r:o_m].astype(BF16),
        "w_qm": wi[:, o_m:].astype(BF16),
        "w_lru_conv": w_lru_conv[l],
        "b_lru_conv": row(b_lru_conv[l]),
        "w_ax": jnp.concatenate([_block_diag(w_lru_a[l]), _block_diag(w_lru_x[l])],
                                axis=1).astype(BF16),
        "b_ax": jnp.concatenate([row(b_lru_a[l]), row(b_lru_x[l])], axis=1),
        "lam": row(lru_lambda[l]),
        "w_gate": w_gate[l].astype(BF16),
        "b_gate": row(b_gate[l]),
        "w_o_fox": w_o_fox[l].astype(BF16),
        "w_o_lru": w_o_lru[l].astype(BF16),
        "w_o_mem": w_o_mem[l].astype(BF16),
        "w_out": w_out[l].astype(BF16),
        "g_ffn": row(g_ffn[l]),
        "w_ffn_gate": w_ffn_gate[l].astype(BF16),
        "w_ffn_up": w_ffn_up[l].astype(BF16),
        "w_ffn_conv": w_ffn_conv[l],
        "b_ffn_conv": row(b_ffn_conv[l]),
        "w_ffn_down": w_ffn_down[l].astype(BF16),
        "g_final": row(g_final),
    }


def kernel(x_prompt, x_sample, mem_prompt, cache_k, cache_v, cache_logf, cache_mem_k, cache_mem_v,
           state_lru_h, state_lru_conv, state_ffn_conv, page_table,
           g_mix, w_in, b_f, w_o_fox, w_lru_conv, b_lru_conv, w_lru_a, b_lru_a, w_lru_x, b_lru_x,
           lru_lambda, w_o_lru, g_mem, w_mem_kv, w_o_mem, w_gate, b_gate, w_out,
           g_ffn, w_ffn_gate, w_ffn_up, w_ffn_conv, b_ffn_conv, w_ffn_down, g_final):
    depth = w_in.shape[0]
    assert depth == 1, "the final norm is fused into the single layer"
    l = 0
    B, S, D = x_prompt.shape
    DB, T, _ = x_sample.shape
    n_pool = cache_k.shape[1]
    w = _prep_weights(l, g_mix, w_in, b_f, w_o_fox, w_lru_conv, b_lru_conv, w_lru_a, b_lru_a,
                      w_lru_x, b_lru_x, lru_lambda, w_o_lru, w_o_mem, w_gate, b_gate, w_out,
                      g_ffn, w_ffn_gate, w_ffn_up, w_ffn_conv, b_ffn_conv, w_ffn_down, g_final)

    mk_p, mv_p = _mem_kv(mem_prompt, g_mem[l].reshape(1, -1), w_mem_kv[l].astype(BF16))
    (k_p, v_p, logf_p, ck_p, qb, kb, vb, olru_p, omem_p, xtail_p, htail_p) = _proj_prompt(
        x_prompt, mk_p, mv_p, w)
    ofox_p = _fox_prompt(qb, kb, vb, ck_p)
    y_p, gtail_p = _merge_ffn_prompt(x_prompt, ofox_p, olru_p, omem_p, w)

    x2 = x_sample.reshape(DB * T, D)
    cstate_tm = jnp.swapaxes(state_lru_conv[l], 0, 1)
    fstate_tm = jnp.swapaxes(state_ffn_conv[l], 0, 1)
    (q_s, k_s, v_s, logf_s, qm_s, olru_s, xtail_s, hlast_s) = _proj_sample(
        x2, cstate_tm, state_lru_h[l], w)
    omem_s = _mem_sample(qm_s, cache_mem_k[l], cache_mem_v[l])
    cache_kt = jnp.transpose(cache_k[l], (0, 2, 3, 1)).reshape(n_pool, FOX_W, PAGE)
    cache_vt = jnp.transpose(cache_v[l], (0, 2, 3, 1)).reshape(n_pool, FOX_W, PAGE)
    cache_ft = jnp.transpose(cache_logf[l], (0, 2, 1))
    ofox_s = _fox_sample(page_table, q_s, k_s, v_s, logf_s.reshape(DB, T, N_HEADS),
                         cache_kt, cache_vt, cache_ft)
    y_s, gtail_s = _merge_ffn_sample(x2, ofox_s, olru_s, omem_s, fstate_tm, w)

    heads = lambda a, n: a.reshape(1, n, -1, N_HEADS, HEAD_DIM)
    mem_heads = lambda a: a.reshape(1, B, -1, MEM_HEADS, MEM_DIM)
    return (
        y_p,
        y_s.reshape(DB, T, D),
        heads(k_p, B), heads(v_p, B), logf_p[None],
        mem_heads(mk_p), mem_heads(mv_p),
        htail_p[None, :, -1, :],
        xtail_p[None, :, SUBLANES - (CONV_LRU - 1):, :],
        gtail_p[None, :, SUBLANES - (CONV_FFN - 1):, :],
        heads(k_s, DB), heads(v_s, DB), logf_s.reshape(1, DB, T, N_HEADS),
        hlast_s[None],
        jnp.swapaxes(xtail_s, 0, 1)[None],
        jnp.swapaxes(gtail_s, 0, 1)[None],
    )
```

```python
import functools

import jax
import jax.numpy as jnp
from jax import lax
from jax.experimental import pallas as pl
from jax.experimental.pallas import tpu as pltpu

F32 = jnp.float32
BF16 = jnp.bfloat16

EPS = 1e-6
LRU_C = 8.0
NEG_BIG = -1e30
LANES = 128
SUBLANES = 8
VMEM_LIMIT = 56 * 1024 * 1024

N_HEADS = 8
HEAD_DIM = 64
FOX_W = N_HEADS * HEAD_DIM
LRU_W = 512
MEM_HEADS = 4
MEM_DIM = 128
MEM_W = MEM_HEADS * MEM_DIM
CONV_LRU = 4
CONV_FFN = 3
PAGE = 128

TM_PROJ = 256
TM_FFN = 256
TQ = 512
FF_CHUNK = 1024
PAGES_PER_STEP = 16
FETCH_SLOTS = 3
MEM_BATCH = 8
SAMPLE_ROWS = 256


def _rms(x, g):
    return x * lax.rsqrt(jnp.mean(x * x, axis=-1, keepdims=True) + EPS) * g


def _log_sigmoid(x):
    return jnp.minimum(x, 0.0) - jnp.log1p(jnp.exp(-jnp.abs(x)))


def _gelu(x):
    return 0.5 * x * (1.0 + jnp.tanh(0.7978845608028654 * (x + 0.044715 * (x * x * x))))


def _dot(a, b):
    return jnp.dot(a, b, preferred_element_type=F32)


def _dot_nt(a, b):
    return lax.dot_general(a, b, (((1,), (1,)), ((), ())), preferred_element_type=F32)


def _lru_coeffs(xc, wax_ref, bax_ref, lam_ref):
    gates = _dot(xc.astype(BF16), wax_ref[...]) + bax_ref[...]
    r = jax.nn.sigmoid(gates[:, :LRU_W])
    i = jax.nn.sigmoid(gates[:, LRU_W:])
    log_a = LRU_C * r * _log_sigmoid(lam_ref[...])
    a = jnp.exp(log_a)
    th = jnp.tanh(log_a)
    gated = jnp.sqrt(-2.0 * th / (1.0 - th)) * i * xc
    return a, gated


def _mem_attend_head(q, mk, mv):
    s = _dot_nt(q, mk)
    m = jnp.max(s, axis=-1, keepdims=True)
    p = jnp.exp(s - m)
    l = jnp.sum(p, axis=-1, keepdims=True)
    return _dot(p.astype(BF16), mv) / l


def _lane_cumsum(x, width):
    lane = lax.broadcasted_iota(jnp.int32, x.shape, x.ndim - 1)
    d = 1
    while d < width:
        x = x + jnp.where(lane >= d, pltpu.roll(x, d, axis=x.ndim - 1), 0.0)
        d *= 2
    return x


def _park(scr, x):
    for c in range(scr.shape[0]):
        scr[c] = x[:, c * LANES:(c + 1) * LANES]


def _unpark(scr):
    return jnp.concatenate([scr[c] for c in range(scr.shape[0])], axis=1)


def _token_rows(scr, t, nseq, ntok):
    return jnp.concatenate([scr[c, pl.ds(t, nseq, stride=ntok), :] for c in range(scr.shape[0])],
                           axis=1)


def _set_token_rows(scr, t, nseq, ntok, x):
    for c in range(scr.shape[0]):
        scr[c, pl.ds(t, nseq, stride=ntok), :] = x[:, c * LANES:(c + 1) * LANES]


def _const_spec(shape):
    nd = len(shape)
    return pl.BlockSpec(shape, lambda *_: (0,) * nd, pipeline_mode=pl.Buffered(1))


def _mem_kv_kernel(mem_ref, g_ref, w_ref, mk_ref, mv_ref):
    u = _rms(mem_ref[0], g_ref[...]).astype(BF16)
    z = _dot(u, w_ref[...])
    mk_ref[0] = z[:, :MEM_W]
    mv_ref[0] = z[:, MEM_W:]


def _mem_kv(mem, g_mem, w_mem_kv):
    B, M, D = mem.shape
    return pl.pallas_call(
        _mem_kv_kernel,
        grid=(B,),
        in_specs=[pl.BlockSpec((1, M, D), lambda b: (b, 0, 0)),
                  _const_spec((1, D)), _const_spec((D, 2 * MEM_W))],
        out_specs=[pl.BlockSpec((1, M, MEM_W), lambda b: (b, 0, 0))] * 2,
        out_shape=[jax.ShapeDtypeStruct((B, M, MEM_W), F32)] * 2,
        compiler_params=pltpu.CompilerParams(dimension_semantics=("arbitrary",),
                                             vmem_limit_bytes=VMEM_LIMIT),
        name="mem_kv",
    )(mem, g_mem, w_mem_kv)


def _proj_prompt_tile(t, x_ref, gmix_ref, wqkv_ref, wf_ref, bf_ref, wrnn_ref, wqm_ref,
                      wconv_ref, bconv_ref, wax_ref, bax_ref, lam_ref, mk_ref, mv_ref,
                      k_ref, v_ref, logf_ref, ck_ref, qb_ref, kb_ref, vb_ref,
                      olru_ref, omem_ref, xtail_ref, htail_ref,
                      xext_scr, a_scr, g_scr, h_scr, c_scr):
    tm = x_ref.shape[1]
    pad = a_scr.shape[0] - tm

    @pl.when(t == 0)
    def _():
        xext_scr[0:SUBLANES, :] = jnp.zeros((SUBLANES, LRU_W), F32)
        h_scr[...] = jnp.zeros_like(h_scr)
        c_scr[...] = jnp.zeros_like(c_scr)
        a_scr[0:pad, :] = jnp.ones((pad, LRU_W), F32)
        g_scr[0:pad, :] = jnp.zeros((pad, LRU_W), F32)

    u = _rms(x_ref[0], gmix_ref[...]).astype(BF16)

    zqkv = _dot(u, wqkv_ref[...])
    k_ref[0] = zqkv[:, FOX_W:2 * FOX_W]
    v_ref[0] = zqkv[:, 2 * FOX_W:]
    qb_ref[0] = zqkv[:, :FOX_W].astype(BF16)
    kb_ref[0] = zqkv[:, FOX_W:2 * FOX_W].astype(BF16)
    vb_ref[0] = zqkv[:, 2 * FOX_W:].astype(BF16)

    logf = _log_sigmoid(_dot(u, wf_ref[...]) + bf_ref[...])
    logf_ref[0] = logf[:, :N_HEADS]
    c = _lane_cumsum(logf.T[:N_HEADS, :], tm) + c_scr[:, 0:1]
    ck_ref[0] = c
    c_scr[...] = jnp.broadcast_to(c[:, tm - 1:tm], c_scr.shape)

    zr = _dot(u, wrnn_ref[...])
    xr = zr[:, :LRU_W]
    xext_scr[SUBLANES:SUBLANES + tm, :] = xr
    xtail_ref[0] = xr[tm - SUBLANES:, :]
    xc = bconv_ref[...] + wconv_ref[CONV_LRU - 1:CONV_LRU, :] * xr
    for j in range(CONV_LRU - 1):
        off = SUBLANES - (CONV_LRU - 1) + j
        xc = xc + wconv_ref[j:j + 1, :] * xext_scr[off:off + tm, :]
    xext_scr[0:SUBLANES, :] = xr[tm - SUBLANES:, :]

    a, gated = _lru_coeffs(xc, wax_ref, bax_ref, lam_ref)
    d = 1
    while d < tm:
        a_scr[pad:pad + tm, :] = a
        g_scr[pad:pad + tm, :] = gated
        gated = gated + a * g_scr[pad - d:pad - d + tm, :]
        a = a * a_scr[pad - d:pad - d + tm, :]
        d *= 2
    hs = gated + a * h_scr[0:1, :]
    h_scr[...] = jnp.broadcast_to(hs[tm - 1:tm, :], h_scr.shape)
    htail_ref[0] = hs[tm - SUBLANES:, :]
    olru_ref[0] = (hs * _gelu(zr[:, LRU_W:])).astype(BF16)

    zq = _dot(u, wqm_ref[...]) * (MEM_DIM ** -0.5)
    for h in range(MEM_HEADS):
        sl = slice(h * MEM_DIM, (h + 1) * MEM_DIM)
        o = _mem_attend_head(zq[:, sl].astype(BF16), mk_ref[0, :, sl].astype(BF16),
                             mv_ref[0, :, sl].astype(BF16))
        omem_ref[0, :, sl] = o.astype(BF16)


def _proj_prompt_call_parts(x, mk, mv, w, tile_of):
    B, S, D = x.shape
    M = mk.shape[1]
    tm = TM_PROJ

    def spec(block, place):
        return pl.BlockSpec(block, lambda *g: place(*tile_of(*g)))

    row = lambda width: spec((1, tm, width), lambda b, t: (b, t, 0))
    per_b = lambda rows, width: spec((1, rows, width), lambda b, t: (b, 0, 0))
    consts = [w["g_mix"], w["w_qkv"], w["w_f"], w["b_f"], w["w_rnn"], w["w_qm"],
              w["w_lru_conv"], w["b_lru_conv"], w["w_ax"], w["b_ax"], w["lam"]]
    out_shape = [
        jax.ShapeDtypeStruct((B, S, FOX_W), F32),
        jax.ShapeDtypeStruct((B, S, FOX_W), F32),
        jax.ShapeDtypeStruct((B, S, N_HEADS), F32),
        jax.ShapeDtypeStruct((B, N_HEADS, S), F32),
        jax.ShapeDtypeStruct((B, S, FOX_W), BF16),
        jax.ShapeDtypeStruct((B, S, FOX_W), BF16),
        jax.ShapeDtypeStruct((B, S, FOX_W), BF16),
        jax.ShapeDtypeStruct((B, S, LRU_W), BF16),
        jax.ShapeDtypeStruct((B, S, MEM_W), BF16),
        jax.ShapeDtypeStruct((B, SUBLANES, LRU_W), F32),
        jax.ShapeDtypeStruct((B, SUBLANES, LRU_W), F32),
    ]
    out_specs = [row(FOX_W), row(FOX_W), row(N_HEADS),
                 spec((1, N_HEADS, tm), lambda b, t: (b, 0, t)),
                 row(FOX_W), row(FOX_W), row(FOX_W), row(LRU_W), row(MEM_W),
                 per_b(SUBLANES, LRU_W), per_b(SUBLANES, LRU_W)]
    pad = tm // 2
    in_specs = [row(D)] + [_const_spec(c.shape) for c in consts] + [per_b(M, MEM_W)] * 2
    scratch_shapes = [pltpu.VMEM((tm + SUBLANES, LRU_W), F32),
                      pltpu.VMEM((tm + pad, LRU_W), F32),
                      pltpu.VMEM((tm + pad, LRU_W), F32),
                      pltpu.VMEM((SUBLANES, LRU_W), F32),
                      pltpu.VMEM((N_HEADS, LANES), F32)]
    return [x, *consts, mk, mv], in_specs, out_specs, out_shape, scratch_shapes


def _fox_prompt_kernel(q_ref, k_ref, v_ref, ck_ref, o_ref, s_scr, p_scr, m_scr, acc_scr):
    tq = q_ref.shape[1]
    i = pl.program_id(2)
    q2 = q_ref[0]
    first = lax.broadcasted_iota(jnp.int32, q2.shape, 1) < HEAD_DIM
    zero = jnp.zeros_like(q2)
    qs = jnp.concatenate([jnp.where(first, q2, zero), jnp.where(first, zero, q2)], axis=0)
    ones = jnp.ones((tq, LANES), BF16)

    def logits(j):
        return _dot_nt(qs, k_ref[0, pl.ds(pl.multiple_of(j * tq, tq), tq), :])

    def weighted_values(p, j):
        v_aug = jnp.concatenate([v_ref[0, pl.ds(pl.multiple_of(j * tq, tq), tq), :], ones], axis=1)
        return _dot(p, v_aug)

    def softmax_block(j, slot, pv, masked):
        start = pl.multiple_of(j * tq, tq)
        if masked:
            causal = (lax.broadcasted_iota(jnp.int32, (tq, tq), 1)
                      <= lax.broadcasted_iota(jnp.int32, (tq, tq), 0))
        for hh in range(2):
            rows = slice(hh * tq, (hh + 1) * tq)
            s = s_scr[slot, rows, :] - ck_ref[0, 0, hh:hh + 1, pl.ds(start, tq)]
            if masked:
                s = jnp.where(causal, s, NEG_BIG)
            m_new = jnp.max(s, axis=-1, keepdims=True)
            if pv is None:
                acc_scr[rows, :] = jnp.zeros((tq, acc_scr.shape[1]), F32)
            else:
                m_old = m_scr[rows, :]
                m_new = jnp.maximum(m_old, m_new)
                acc_scr[rows, :] = jnp.exp(m_old - m_new) * (acc_scr[rows, :] + pv[rows, :])
            p_scr[slot, rows, :] = jnp.exp(s - m_new).astype(BF16)
            m_scr[rows, :] = m_new

    def step(t, slot):
        other = 1 - slot
        pv = weighted_values(p_scr[other], t - 1)
        s_scr[other] = logits(t + 1)
        softmax_block(t, slot, pv, False)

    def finish(slot, pv):
        softmax_block(i, slot, pv, True)
        acc = acc_scr[...] + weighted_values(p_scr[slot], i)
        out = acc[:, :LANES] / acc[:, LANES:]
        o_ref[0] = jnp.where(first, out[:tq, :], out[tq:, :]).astype(o_ref.dtype)

    s_scr[0] = logits(0)

    @pl.when(i == 0)
    def _():
        finish(0, None)

    @pl.when(i > 0)
    def _():
        s_scr[1] = logits(1)
        softmax_block(0, 0, None, False)

        def two_steps(tt, carry):
            step(2 * tt + 1, 1)
            step(2 * tt + 2, 0)
            return carry

        lax.fori_loop(0, (i - 1) // 2, two_steps, 0)

        @pl.when(i % 2 == 0)
        def _():
            step(i - 1, 1)
            finish(0, weighted_values(p_scr[1], i - 1))

        @pl.when(i % 2 == 1)
        def _():
            finish(1, weighted_values(p_scr[0], i - 1))


def _fox_prompt(qb, kb, vb, ck):
    B, S, W = qb.shape
    npair = W // LANES
    ck4 = ck.reshape(B, npair, 2, S)
    return pl.pallas_call(
        _fox_prompt_kernel,
        grid=(B, npair, S // TQ),
        in_specs=[pl.BlockSpec((1, TQ, LANES), lambda b, h, i: (b, i, h)),
                  pl.BlockSpec((1, S, LANES), lambda b, h, i: (b, 0, h)),
                  pl.BlockSpec((1, S, LANES), lambda b, h, i: (b, 0, h)),
                  pl.BlockSpec((1, 1, 2, S), lambda b, h, i: (b, h, 0, 0))],
        out_specs=pl.BlockSpec((1, TQ, LANES), lambda b, h, i: (b, i, h)),
        out_shape=jax.ShapeDtypeStruct((B, S, W), BF16),
        scratch_shapes=[pltpu.VMEM((2, 2 * TQ, TQ), F32), pltpu.VMEM((2, 2 * TQ, TQ), BF16),
                        pltpu.VMEM((2 * TQ, 1), F32), pltpu.VMEM((2 * TQ, 2 * LANES), F32)],
        compiler_params=pltpu.CompilerParams(
            dimension_semantics=("arbitrary", "arbitrary", "arbitrary"),
            vmem_limit_bytes=VMEM_LIMIT),
        name="fox_prompt",
    )(qb, kb, vb, ck4)


def _merge(x, branches, gmix_ref, wgate_ref, bgate_ref, wout_ref):
    D = x.shape[1]
    u = _rms(x, gmix_ref[...]).astype(BF16)
    merged = None
    for j, (o, wo_ref) in enumerate(branches):
        sl = slice(j * D, (j + 1) * D)
        gate = jax.nn.sigmoid(_dot(u, wgate_ref[:, sl]) + bgate_ref[:, sl])
        term = gate * _dot(o, wo_ref[...])
        merged = term if merged is None else merged + term
    return x + _dot(merged.astype(BF16), wout_ref[...])


def _merge_ffn_prompt_kernel(x_ref, ofox_ref, olru_ref, omem_ref, gmix_ref, wgate_ref, bgate_ref,
                             wofox_ref, wolru_ref, womem_ref, wout_ref, gffn_ref, wfg_ref, wfu_ref,
                             wfc_ref, bfc_ref, wfd_ref, gfin_ref,
                             y_ref, gtail_ref, gext_scr):
    tm = x_ref.shape[1]
    dff = wfg_ref.shape[1]
    t = pl.program_id(1)

    @pl.when(t == 0)
    def _():
        gext_scr[0:SUBLANES, :] = jnp.zeros((SUBLANES, dff), F32)

    h = _merge(x_ref[0], [(ofox_ref[0], wofox_ref), (olru_ref[0], wolru_ref),
                          (omem_ref[0], womem_ref)], gmix_ref, wgate_ref, bgate_ref, wout_ref)
    v2 = _rms(h, gffn_ref[...]).astype(BF16)
    y = h
    for c in range(dff // FF_CHUNK):
        sl = slice(c * FF_CHUNK, (c + 1) * FF_CHUNK)
        g = _dot(v2, wfg_ref[:, sl])
        gext_scr[SUBLANES:SUBLANES + tm, sl] = g
        gc = bfc_ref[:, sl] + wfc_ref[CONV_FFN - 1:CONV_FFN, sl] * g
        for j in range(CONV_FFN - 1):
            off = SUBLANES - (CONV_FFN - 1) + j
            gc = gc + wfc_ref[j:j + 1, sl] * gext_scr[off:off + tm, sl]
        act = (_gelu(gc) * _dot(v2, wfu_ref[:, sl])).astype(BF16)
        y = y + _dot(act, wfd_ref[sl, :])
    tail = gext_scr[tm:tm + SUBLANES, :]
    gtail_ref[0] = tail
    gext_scr[0:SUBLANES, :] = tail
    y_ref[0] = _rms(y, gfin_ref[...])


def _merge_ffn_consts(w):
    return [w["g_mix"], w["w_gate"], w["b_gate"], w["w_o_fox"], w["w_o_lru"], w["w_o_mem"],
            w["w_out"], w["g_ffn"], w["w_ffn_gate"], w["w_ffn_up"], w["w_ffn_conv"],
            w["b_ffn_conv"], w["w_ffn_down"], w["g_final"]]


def _merge_ffn_prompt(x, ofox, olru, omem, w):
    B, S, D = x.shape
    dff = w["w_ffn_gate"].shape[1]
    tm = TM_FFN
    row = lambda width: pl.BlockSpec((1, tm, width), lambda b, t: (b, t, 0))
    consts = _merge_ffn_consts(w)
    return pl.pallas_call(
        _merge_ffn_prompt_kernel,
        grid=(B, S // tm),
        in_specs=[row(D), row(FOX_W), row(LRU_W), row(MEM_W)]
                 + [_const_spec(c.shape) for c in consts],
        out_specs=[row(D), pl.BlockSpec((1, SUBLANES, dff), lambda b, t: (b, 0, 0))],
        out_shape=[jax.ShapeDtypeStruct((B, S, D), F32),
                   jax.ShapeDtypeStruct((B, SUBLANES, dff), F32)],
        scratch_shapes=[pltpu.VMEM((tm + SUBLANES, dff), F32)],
        compiler_params=pltpu.CompilerParams(dimension_semantics=("arbitrary", "arbitrary"),
                                             vmem_limit_bytes=VMEM_LIMIT),
        name="merge_ffn_prompt",
    )(x, ofox, olru, omem, *consts)


def _proj_sample_kernel(x_ref, gmix_ref, wqkv_ref, wf_ref, bf_ref, wrnn_ref, wqm_ref,
                        wconv_ref, bconv_ref, wax_ref, bax_ref, lam_ref, cstate_ref, h0_ref,
                        q_ref, k_ref, v_ref, logf_ref, qm_ref, olru_ref, xtail_ref, hlast_ref,
                        x_scr, h_scr):
    nb = h0_ref.shape[0]
    nt = x_ref.shape[0] // nb
    u = _rms(x_ref[...], gmix_ref[...]).astype(BF16)
    zqkv = _dot(u, wqkv_ref[...])
    q_ref[...] = zqkv[:, :FOX_W]
    k_ref[...] = zqkv[:, FOX_W:2 * FOX_W]
    v_ref[...] = zqkv[:, 2 * FOX_W:]
    logf = _log_sigmoid(_dot(u, wf_ref[...]) + bf_ref[...])
    logf_ref[...] = logf[:, :N_HEADS]
    qm_ref[...] = _dot(u, wqm_ref[...]) * (MEM_DIM ** -0.5)

    zr = _dot(u, wrnn_ref[...])
    _park(x_scr, zr[:, :LRU_W])
    xs = [cstate_ref[j] for j in range(CONV_LRU - 1)]
    xs += [_token_rows(x_scr, t, nb, nt) for t in range(nt)]
    h = h0_ref[...]
    for t in range(nt):
        xc = bconv_ref[...]
        for j in range(CONV_LRU):
            xc = xc + wconv_ref[j:j + 1, :] * xs[t + j]
        a, gated = _lru_coeffs(xc, wax_ref, bax_ref, lam_ref)
        h = a * h + gated
        _set_token_rows(h_scr, t, nb, nt, h)
    hlast_ref[...] = h
    for j in range(CONV_LRU - 1):
        xtail_ref[j] = xs[nt + j]
    olru_ref[...] = (_unpark(h_scr) * _gelu(zr[:, LRU_W:])).astype(BF16)


def _proj_sample(x2, cstate_tm, h0, w):
    R, D = x2.shape
    nb = h0.shape[0]
    nt = R // nb
    rb = SAMPLE_ROWS
    sb = rb // nt
    consts = [w["g_mix"], w["w_qkv"], w["w_f"], w["b_f"], w["w_rnn"], w["w_qm"],
              w["w_lru_conv"], w["b_lru_conv"], w["w_ax"], w["b_ax"], w["lam"]]
    row = lambda width: pl.BlockSpec((rb, width), lambda i: (i, 0))
    state = pl.BlockSpec((CONV_LRU - 1, sb, LRU_W), lambda i: (0, i, 0))
    seq = pl.BlockSpec((sb, LRU_W), lambda i: (i, 0))
    out_shape = [
        jax.ShapeDtypeStruct((R, FOX_W), F32),
        jax.ShapeDtypeStruct((R, FOX_W), F32),
        jax.ShapeDtypeStruct((R, FOX_W), F32),
        jax.ShapeDtypeStruct((R, N_HEADS), F32),
        jax.ShapeDtypeStruct((R, MEM_W), F32),
        jax.ShapeDtypeStruct((R, LRU_W), BF16),
        jax.ShapeDtypeStruct((CONV_LRU - 1, nb, LRU_W), F32),
        jax.ShapeDtypeStruct((nb, LRU_W), F32),
    ]
    return pl.pallas_call(
        _proj_sample_kernel,
        grid=(R // rb,),
        in_specs=[row(D)] + [_const_spec(c.shape) for c in consts] + [state, seq],
        out_specs=[row(FOX_W), row(FOX_W), row(FOX_W), row(N_HEADS), row(MEM_W), row(LRU_W),
                   state, seq],
        out_shape=out_shape,
        scratch_shapes=[pltpu.VMEM((LRU_W // LANES, rb, LANES), F32)] * 2,
        compiler_params=pltpu.CompilerParams(dimension_semantics=("arbitrary",),
                                             vmem_limit_bytes=VMEM_LIMIT),
        name="proj_sample",
    )(x2, *consts, cstate_tm, h0)


def _mem_sample_kernel(q_ref, mk_ref, mv_ref, o_ref):
    nb = mk_ref.shape[0]
    nt = q_ref.shape[0] // nb
    nrow = MEM_HEADS * nt
    row_head = lax.broadcasted_iota(jnp.int32, (nrow, mk_ref.shape[1]), 0) // nt
    col_head = lax.broadcasted_iota(jnp.int32, (nrow, mk_ref.shape[1]), 1) % MEM_HEADS
    own = row_head == col_head
    for b in range(nb):
        rows = slice(b * nt, (b + 1) * nt)
        q = jnp.concatenate([q_ref[rows, h * MEM_DIM:(h + 1) * MEM_DIM] for h in range(MEM_HEADS)],
                            axis=0).astype(BF16)
        s = jnp.where(own, _dot_nt(q, mk_ref[b].astype(BF16)), NEG_BIG)
        p = jnp.exp(s - jnp.max(s, axis=-1, keepdims=True))
        o = _dot(p.astype(BF16), mv_ref[b].astype(BF16)) / jnp.sum(p, axis=-1, keepdims=True)
        for h in range(MEM_HEADS):
            o_ref[rows, h * MEM_DIM:(h + 1) * MEM_DIM] = o[h * nt:(h + 1) * nt, :]


def _mem_sample(qm, mk, mv):
    R, W = qm.shape
    DB, MH, Dm = mk.shape
    nt = R // DB
    nb = MEM_BATCH
    kv_spec = pl.BlockSpec((nb, MH, Dm), lambda i: (i, 0, 0))
    return pl.pallas_call(
        _mem_sample_kernel,
        grid=(DB // nb,),
        in_specs=[pl.BlockSpec((nb * nt, W), lambda i: (i, 0)), kv_spec, kv_spec],
        out_specs=pl.BlockSpec((nb * nt, W), lambda i: (i, 0)),
        out_shape=jax.ShapeDtypeStruct((R, W), F32),
        compiler_params=pltpu.CompilerParams(dimension_semantics=("arbitrary",),
                                             vmem_limit_bytes=VMEM_LIMIT),
        name="mem_sample",
    )(qm, mk, mv)


def _fox_sample_kernel(pt_ref, q_ref, kn_ref, vn_ref, lfn_ref, k_hbm, v_hbm, f_hbm, o_ref,
                       k_buf, v_buf, f_buf, k_sem, v_sem, f_sem,
                       kb_scr, vb_scr, qbd_scr, m_scr, l_scr, acc_scr, c_scr, pack_scr):
    npg = PAGES_PER_STEP
    nslot = k_buf.shape[0]
    nt = q_ref.shape[0]
    nrow = N_HEADS * nt
    c = pl.program_id(1)
    nch = pl.num_programs(1)
    step = pl.program_id(0) * nch + c
    last = pl.num_programs(0) * nch - 1
    lane = lax.broadcasted_iota(jnp.int32, (nt, FOX_W), 1)
    streams = ((k_hbm, k_buf, k_sem), (v_hbm, v_buf, v_sem), (f_hbm, f_buf, f_sem))

    def page_copies(slot, pages):
        return [pltpu.make_async_copy(hbm.at[pages[p]], buf.at[slot, p], sem.at[slot])
                for hbm, buf, sem in streams for p in range(npg)]

    def fetch(s):
        src = jnp.minimum(s, last)
        row = src // nch
        col = (src % nch) * npg
        for cp in page_copies(s % nslot, [pt_ref[row, col + p] for p in range(npg)]):
            cp.start()

    def wait(slot):
        for cp in page_copies(slot, [0] * npg):
            cp.wait()

    @pl.when(step == 0)
    def _():
        for s in range(nslot - 1):
            fetch(jnp.int32(s))

    slot = step % nslot
    wait(slot)
    fetch(step + (nslot - 1))
    k_pages = [k_buf.at[slot, p] for p in range(npg)]
    v_pages = [v_buf.at[slot, p] for p in range(npg)]
    f_pages = [f_buf.at[slot, p] for p in range(npg)]

    @pl.when(c == 0)
    def _():
        q = q_ref[...]
        for h in range(N_HEADS):
            mine = (lane >= h * HEAD_DIM) & (lane < (h + 1) * HEAD_DIM)
            qbd_scr[h * nt:(h + 1) * nt, :] = jnp.where(mine, q, 0.0).astype(BF16)
        m_scr[...] = jnp.full_like(m_scr, NEG_BIG)
        l_scr[...] = jnp.zeros_like(l_scr)
        acc_scr[...] = jnp.zeros_like(acc_scr)
        c_scr[...] = jnp.zeros_like(c_scr)

    def update(s, pv):
        m = m_scr[...]
        m_new = jnp.maximum(m, jnp.max(s, axis=-1, keepdims=True))
        alpha = jnp.exp(m - m_new)
        p = jnp.exp(s - m_new)
        l_scr[...] = alpha * l_scr[...] + jnp.sum(p, axis=-1, keepdims=True)
        acc_scr[...] = alpha * acc_scr[...] + pv(p.astype(BF16))
        m_scr[...] = m_new

    def head_rows(cb):
        return jnp.concatenate(
            [jnp.broadcast_to(cb[h:h + 1, :], (nt, cb.shape[1])) for h in range(N_HEADS)], axis=0)

    z = _lane_cumsum(jnp.concatenate([f_pages[p][...] for p in range(npg)], axis=0), PAGE)
    tiles = [z[p * N_HEADS:(p + 1) * N_HEADS, :] for p in range(npg)]
    totals = [jnp.broadcast_to(t[:, PAGE - 1:PAGE], t.shape) for t in tiles]
    off = c_scr[...]
    for p in range(npg):
        tiles[p] = tiles[p] + off
        off = off + totals[p]
    c_scr[...] = off

    for p in range(npg):
        kb_scr[:, p * PAGE:(p + 1) * PAGE] = k_pages[p][...].astype(BF16)
        vb_scr[:, p * PAGE:(p + 1) * PAGE] = v_pages[p][...].astype(BF16)
    s = _dot(qbd_scr[...], kb_scr[...]) - head_rows(jnp.concatenate(tiles, axis=1))
    update(s, lambda p: _dot_nt(p, vb_scr[...]))

    @pl.when(c == pl.num_programs(1) - 1)
    def _():
        zeros = jnp.zeros((PAGE - nt, FOX_W), F32)
        kn = jnp.concatenate([kn_ref[...], zeros], axis=0).astype(BF16)
        vn = jnp.concatenate([vn_ref[...], zeros], axis=0).astype(BF16)
        pack_scr[...] = jnp.zeros_like(pack_scr)
        pack_scr[0:nt, 0:N_HEADS] = lfn_ref[0]
        cn = _lane_cumsum(pack_scr[...].T[:N_HEADS, :], PAGE) + c_scr[...]
        s = _dot_nt(qbd_scr[...], kn) - head_rows(cn)
        tok = lax.broadcasted_iota(jnp.int32, (nrow, PAGE), 0) % nt
        key = lax.broadcasted_iota(jnp.int32, (nrow, PAGE), 1)
        update(jnp.where(key <= tok, s, NEG_BIG), lambda p: _dot(p, vn))
        acc = acc_scr[...] / l_scr[...]
        out = acc[0:nt, :]
        for h in range(1, N_HEADS):
            out = jnp.where(lane >= h * HEAD_DIM, acc[h * nt:(h + 1) * nt, :], out)
        o_ref[...] = out

    @pl.when(step == last)
    def _():
        for ahead in range(1, nslot):
            wait((step + ahead) % nslot)


N_FOX_SAMPLE_IN = 7
N_FOX_SAMPLE_SCRATCH = 14
N_PROJ_PROMPT_IN = 14
N_PROJ_PROMPT_OUT = 11


def _fox_sample_proj_prompt_kernel(steps_per_tile, tiles_per_seq, pt_ref, *refs):
    bounds = [N_FOX_SAMPLE_IN, N_PROJ_PROMPT_IN, 1, N_PROJ_PROMPT_OUT, N_FOX_SAMPLE_SCRATCH]
    parts, at = [], 0
    for n in bounds:
        parts.append(refs[at:at + n])
        at += n
    fox_in, proj_in, fox_out, proj_out, fox_scratch = parts
    proj_scratch = refs[at:]
    _fox_sample_kernel(pt_ref, *fox_in, *fox_out, *fox_scratch)
    step = pl.program_id(0) * pl.num_programs(1) + pl.program_id(1)

    @pl.when(step % steps_per_tile == 0)
    def _():
        _proj_prompt_tile((step // steps_per_tile) % tiles_per_seq,
                          *proj_in, *proj_out, *proj_scratch)


def _fox_sample_proj_prompt(page_table, q, k_new, v_new, logf_new, cache_kt, cache_vt, cache_ft,
                            x, mk, mv, w):
    DB, npages = page_table.shape
    R, W = q.shape
    nt = R // DB
    npg = PAGES_PER_STEP
    nk = npg * PAGE
    nch = npages // npg
    nslot = FETCH_SLOTS
    tiles_per_seq = x.shape[1] // TM_PROJ
    steps_per_tile = (DB * nch) // (x.shape[0] * tiles_per_seq)
    assert steps_per_tile * x.shape[0] * tiles_per_seq == DB * nch

    def tile_of(b, c, pt):
        tile = (b * nch + c) // steps_per_tile
        return tile // tiles_per_seq, tile % tiles_per_seq

    proj_args, proj_in_specs, proj_out_specs, proj_out_shape, proj_scratch = (
        _proj_prompt_call_parts(x, mk, mv, w, tile_of))
    assert (len(proj_args), len(proj_out_specs)) == (N_PROJ_PROMPT_IN, N_PROJ_PROMPT_OUT)
    tok = pl.BlockSpec((nt, W), lambda b, c, pt: (b, 0))
    hbm = pl.BlockSpec(memory_space=pl.ANY)
    fox_in_specs = [tok, tok, tok, pl.BlockSpec((1, nt, N_HEADS), lambda b, c, pt: (b, 0, 0)),
                    hbm, hbm, hbm]
    fox_scratch = [pltpu.VMEM((nslot, npg, W, PAGE), F32),
                   pltpu.VMEM((nslot, npg, W, PAGE), F32),
                   pltpu.VMEM((nslot, npg, N_HEADS, PAGE), F32),
                   pltpu.SemaphoreType.DMA((nslot,)), pltpu.SemaphoreType.DMA((nslot,)),
                   pltpu.SemaphoreType.DMA((nslot,)),
                   pltpu.VMEM((W, nk), BF16), pltpu.VMEM((W, nk), BF16),
                   pltpu.VMEM((N_HEADS * nt, W), BF16),
                   pltpu.VMEM((N_HEADS * nt, 1), F32), pltpu.VMEM((N_HEADS * nt, 1), F32),
                   pltpu.VMEM((N_HEADS * nt, W), F32),
                   pltpu.VMEM((N_HEADS, PAGE), F32), pltpu.VMEM((PAGE, LANES), F32)]
    assert (len(fox_in_specs), len(fox_scratch)) == (N_FOX_SAMPLE_IN, N_FOX_SAMPLE_SCRATCH)
    grid_spec = pltpu.PrefetchScalarGridSpec(
        num_scalar_prefetch=1,
        grid=(DB, nch),
        in_specs=fox_in_specs + proj_in_specs,
        out_specs=[tok] + proj_out_specs,
        scratch_shapes=fox_scratch + proj_scratch,
    )
    return pl.pallas_call(
        functools.partial(_fox_sample_proj_prompt_kernel, steps_per_tile, tiles_per_seq),
        grid_spec=grid_spec,
        out_shape=[jax.ShapeDtypeStruct((R, W), F32)] + proj_out_shape,
        compiler_params=pltpu.CompilerParams(dimension_semantics=("arbitrary", "arbitrary"),
                                             vmem_limit_bytes=VMEM_LIMIT),
        name="fox_sample_proj_prompt",
    )(page_table, q, k_new, v_new, logf_new, cache_kt, cache_vt, cache_ft, *proj_args)


def _merge_ffn_sample_kernel(x_ref, ofox_ref, olru_ref, omem_ref, fstate_ref, gmix_ref, wgate_ref,
                             bgate_ref, wofox_ref, wolru_ref, womem_ref, wout_ref, gffn_ref,
                             wfg_ref, wfu_ref, wfc_ref, bfc_ref, wfd_ref, gfin_ref,
                             y_ref, gtail_ref, g_scr, gc_scr):
    nb = fstate_ref.shape[1]
    nt = x_ref.shape[0] // nb
    dff = wfg_ref.shape[1]
    h = _merge(x_ref[...], [(ofox_ref[...].astype(BF16), wofox_ref), (olru_ref[...], wolru_ref),
                            (omem_ref[...].astype(BF16), womem_ref)],
               gmix_ref, wgate_ref, bgate_ref, wout_ref)
    v2 = _rms(h, gffn_ref[...]).astype(BF16)
    y = h
    for c in range(dff // FF_CHUNK):
        sl = slice(c * FF_CHUNK, (c + 1) * FF_CHUNK)
        _park(g_scr, _dot(v2, wfg_ref[:, sl]))
        gs = [fstate_ref[j, :, sl] for j in range(CONV_FFN - 1)]
        gs += [_token_rows(g_scr, t, nb, nt) for t in range(nt)]
        for t in range(nt):
            gc = bfc_ref[:, sl]
            for j in range(CONV_FFN):
                gc = gc + wfc_ref[j:j + 1, sl] * gs[t + j]
            _set_token_rows(gc_scr, t, nb, nt, gc)
        for j in range(CONV_FFN - 1):
            gtail_ref[j, :, sl] = gs[nt + j]
        act = (_gelu(_unpark(gc_scr)) * _dot(v2, wfu_ref[:, sl])).astype(BF16)
        y = y + _dot(act, wfd_ref[sl, :])
    y_ref[...] = _rms(y, gfin_ref[...])


def _merge_ffn_sample(x2, ofox, olru, omem, fstate_tm, w):
    R, D = x2.shape
    nb = fstate_tm.shape[1]
    nt = R // nb
    dff = w["w_ffn_gate"].shape[1]
    rb = SAMPLE_ROWS
    sb = rb // nt
    consts = _merge_ffn_consts(w)
    row = lambda width: pl.BlockSpec((rb, width), lambda i: (i, 0))
    state = pl.BlockSpec((CONV_FFN - 1, sb, dff), lambda i: (0, i, 0))
    return pl.pallas_call(
        _merge_ffn_sample_kernel,
        grid=(R // rb,),
        in_specs=[row(D), row(FOX_W), row(LRU_W), row(MEM_W), state]
                 + [_const_spec(c.shape) for c in consts],
        out_specs=[row(D), state],
        out_shape=[jax.ShapeDtypeStruct((R, D), F32),
                   jax.ShapeDtypeStruct((CONV_FFN - 1, nb, dff), F32)],
        scratch_shapes=[pltpu.VMEM((FF_CHUNK // LANES, rb, LANES), F32)] * 2,
        compiler_params=pltpu.CompilerParams(dimension_semantics=("arbitrary",),
                                             vmem_limit_bytes=VMEM_LIMIT),
        name="merge_ffn_sample",
    )(x2, ofox, olru, omem, fstate_tm, *consts)


def _block_diag(wb):
    nb, bs, _ = wb.shape
    eye = jnp.eye(nb, dtype=wb.dtype)
    return (eye[:, None, :, None] * wb[:, :, None, :]).reshape(nb * bs, nb * bs)


def _prep_weights(l, g_mix, w_in, b_f, w_o_fox, w_lru_conv, b_lru_conv, w_lru_a, b_lru_a, w_lru_x,
                  b_lru_x, lru_lambda, w_o_lru, w_o_mem, w_gate, b_gate, w_out, g_ffn, w_ffn_gate,
                  w_ffn_up, w_ffn_conv, b_ffn_conv, w_ffn_down, g_final):
    wi = w_in[l]
    o_f = 3 * FOX_W
    o_r = o_f + N_HEADS
    o_m = o_r + 2 * LRU_W
    row = lambda v: v.reshape(1, -1)
    w_qkv = jnp.concatenate([wi[:, :FOX_W] * (HEAD_DIM ** -0.5), wi[:, FOX_W:o_f]], axis=1)
    return {
        "g_mix": row(g_mix[l]),
        "w_qkv": w_qkv.astype(BF16),
        "w_f": jnp.pad(wi[:, o_f:o_r], ((0, 0), (0, LANES - N_HEADS))).astype(BF16),
        "b_f": jnp.pad(row(b_f[l]), ((0, 0), (0, LANES - N_HEADS))),
        "w_rnn": wi[:, o_r:o_m].astype(BF16),
        "w_qm": wi[:, o_m:].astype(BF16),
        "w_lru_conv": w_lru_conv[l],
        "b_lru_conv": row(b_lru_conv[l]),
        "w_ax": jnp.concatenate([_block_diag(w_lru_a[l]), _block_diag(w_lru_x[l])],
                                axis=1).astype(BF16),
        "b_ax": jnp.concatenate([row(b_lru_a[l]), row(b_lru_x[l])], axis=1),
        "lam": row(lru_lambda[l]),
        "w_gate": w_gate[l].astype(BF16),
        "b_gate": row(b_gate[l]),
        "w_o_fox": w_o_fox[l].astype(BF16),
        "w_o_lru": w_o_lru[l].astype(BF16),
        "w_o_mem": w_o_mem[l].astype(BF16),
        "w_out": w_out[l].astype(BF16),
        "g_ffn": row(g_ffn[l]),
        "w_ffn_gate": w_ffn_gate[l].astype(BF16),
        "w_ffn_up": w_ffn_up[l].astype(BF16),
        "w_ffn_conv": w_ffn_conv[l],
        "b_ffn_conv": row(b_ffn_conv[l]),
        "w_ffn_down": w_ffn_down[l].astype(BF16),
        "g_final": row(g_final),
    }


def kernel(x_prompt, x_sample, mem_prompt, cache_k, cache_v, cache_logf, cache_mem_k, cache_mem_v,
           state_lru_h, state_lru_conv, state_ffn_conv, page_table,
           g_mix, w_in, b_f, w_o_fox, w_lru_conv, b_lru_conv, w_lru_a, b_lru_a, w_lru_x, b_lru_x,
           lru_lambda, w_o_lru, g_mem, w_mem_kv, w_o_mem, w_gate, b_gate, w_out,
           g_ffn, w_ffn_gate, w_ffn_up, w_ffn_conv, b_ffn_conv, w_ffn_down, g_final):
    depth = w_in.shape[0]
    assert depth == 1, "the final norm is fused into the single layer"
    l = 0
    B, S, D = x_prompt.shape
    DB, T, _ = x_sample.shape
    n_pool = cache_k.shape[1]
    w = _prep_weights(l, g_mix, w_in, b_f, w_o_fox, w_lru_conv, b_lru_conv, w_lru_a, b_lru_a,
                      w_lru_x, b_lru_x, lru_lambda, w_o_lru, w_o_mem, w_gate, b_gate, w_out,
                      g_ffn, w_ffn_gate, w_ffn_up, w_ffn_conv, b_ffn_conv, w_ffn_down, g_final)

    mk_p, mv_p = _mem_kv(mem_prompt, g_mem[l].reshape(1, -1), w_mem_kv[l].astype(BF16))

    x2 = x_sample.reshape(DB * T, D)
    cstate_tm = jnp.swapaxes(state_lru_conv[l], 0, 1)
    fstate_tm = jnp.swapaxes(state_ffn_conv[l], 0, 1)
    (q_s, k_s, v_s, logf_s, qm_s, olru_s, xtail_s, hlast_s) = _proj_sample(
        x2, cstate_tm, state_lru_h[l], w)
    omem_s = _mem_sample(qm_s, cache_mem_k[l].reshape(DB, -1, MEM_DIM),
                         cache_mem_v[l].reshape(DB, -1, MEM_DIM))
    cache_kt = jnp.transpose(cache_k[l], (0, 2, 3, 1)).reshape(n_pool, FOX_W, PAGE)
    cache_vt = jnp.transpose(cache_v[l], (0, 2, 3, 1)).reshape(n_pool, FOX_W, PAGE)
    cache_ft = jnp.transpose(cache_logf[l], (0, 2, 1))
    (ofox_s, k_p, v_p, logf_p, ck_p, qb, kb, vb, olru_p, omem_p, xtail_p, htail_p) = (
        _fox_sample_proj_prompt(page_table, q_s, k_s, v_s, logf_s.reshape(DB, T, N_HEADS),
                                cache_kt, cache_vt, cache_ft, x_prompt, mk_p, mv_p, w))
    y_s, gtail_s = _merge_ffn_sample(x2, ofox_s, olru_s, omem_s, fstate_tm, w)

    ofox_p = _fox_prompt(qb, kb, vb, ck_p)
    y_p, gtail_p = _merge_ffn_prompt(x_prompt, ofox_p, olru_p, omem_p, w)

    heads = lambda a, n: a.reshape(1, n, -1, N_HEADS, HEAD_DIM)
    mem_heads = lambda a: a.reshape(1, B, -1, MEM_HEADS, MEM_DIM)
    return (
        y_p,
        y_s.reshape(DB, T, D),
        heads(k_p, B), heads(v_p, B), logf_p[None],
        mem_heads(mk_p), mem_heads(mv_p),
        htail_p[None, :, -1, :],
        xtail_p[None, :, SUBLANES - (CONV_LRU - 1):, :],
        gtail_p[None, :, SUBLANES - (CONV_FFN - 1):, :],
        heads(k_s, DB), heads(v_s, DB), logf_s.reshape(1, DB, T, N_HEADS),
        hlast_s[None],
        jnp.swapaxes(xtail_s, 0, 1)[None],
        jnp.swapaxes(gtail_s, 0, 1)[None],
    )
```

```python
import functools

import jax
import jax.numpy as jnp
from jax import lax
from jax.experimental import pallas as pl
from jax.experimental.pallas import tpu as pltpu

F32 = jnp.float32
BF16 = jnp.bfloat16

EPS = 1e-6
LRU_C = 8.0
NEG_BIG = -1e30
LANES = 128
SUBLANES = 8
VMEM_LIMIT = 56 * 1024 * 1024

N_HEADS = 8
HEAD_DIM = 64
FOX_W = N_HEADS * HEAD_DIM
LRU_W = 512
MEM_HEADS = 4
MEM_DIM = 128
MEM_W = MEM_HEADS * MEM_DIM
CONV_LRU = 4
CONV_FFN = 3
PAGE = 128

TM_PROJ = 256
TM_FFN = 256
TQ = 512
FF_CHUNK = 1024
PAGES_PER_STEP = 16
FETCH_SLOTS = 3
PAGE_DMA_PRIORITY = 1
MEM_BATCH = 8
SAMPLE_ROWS = 256


def _rms(x, g):
    return x * lax.rsqrt(jnp.mean(x * x, axis=-1, keepdims=True) + EPS) * g


def _log_sigmoid(x):
    return jnp.minimum(x, 0.0) - jnp.log1p(jnp.exp(-jnp.abs(x)))


def _gelu(x):
    return 0.5 * x * (1.0 + jnp.tanh(0.7978845608028654 * (x + 0.044715 * (x * x * x))))


def _dot(a, b):
    return jnp.dot(a, b, preferred_element_type=F32)


def _dot_nt(a, b):
    return lax.dot_general(a, b, (((1,), (1,)), ((), ())), preferred_element_type=F32)


def _lru_coeffs(xc, wax_ref, bax_ref, lam_ref):
    gates = _dot(xc.astype(BF16), wax_ref[...]) + bax_ref[...]
    r = jax.nn.sigmoid(gates[:, :LRU_W])
    i = jax.nn.sigmoid(gates[:, LRU_W:])
    log_a = LRU_C * r * _log_sigmoid(lam_ref[...])
    a = jnp.exp(log_a)
    th = jnp.tanh(log_a)
    gated = jnp.sqrt(-2.0 * th / (1.0 - th)) * i * xc
    return a, gated


def _mem_attend_head(q, mk, mv):
    s = _dot_nt(q, mk)
    m = jnp.max(s, axis=-1, keepdims=True)
    p = jnp.exp(s - m)
    l = jnp.sum(p, axis=-1, keepdims=True)
    return _dot(p.astype(BF16), mv) / l


def _lane_cumsum(x, width):
    lane = lax.broadcasted_iota(jnp.int32, x.shape, x.ndim - 1)
    d = 1
    while d < width:
        x = x + jnp.where(lane >= d, pltpu.roll(x, d, axis=x.ndim - 1), 0.0)
        d *= 2
    return x


def _park(scr, x):
    for c in range(scr.shape[0]):
        scr[c] = x[:, c * LANES:(c + 1) * LANES]


def _unpark(scr):
    return jnp.concatenate([scr[c] for c in range(scr.shape[0])], axis=1)


def _token_rows(scr, t, nseq, ntok):
    return jnp.concatenate([scr[c, pl.ds(t, nseq, stride=ntok), :] for c in range(scr.shape[0])],
                           axis=1)


def _set_token_rows(scr, t, nseq, ntok, x):
    for c in range(scr.shape[0]):
        scr[c, pl.ds(t, nseq, stride=ntok), :] = x[:, c * LANES:(c + 1) * LANES]


def _const_spec(shape):
    nd = len(shape)
    return pl.BlockSpec(shape, lambda *_: (0,) * nd, pipeline_mode=pl.Buffered(1))


def _mem_kv_kernel(mem_ref, g_ref, w_ref, mk_ref, mv_ref):
    u = _rms(mem_ref[0], g_ref[...]).astype(BF16)
    z = _dot(u, w_ref[...])
    mk_ref[0] = z[:, :MEM_W]
    mv_ref[0] = z[:, MEM_W:]


def _mem_kv(mem, g_mem, w_mem_kv):
    B, M, D = mem.shape
    return pl.pallas_call(
        _mem_kv_kernel,
        grid=(B,),
        in_specs=[pl.BlockSpec((1, M, D), lambda b: (b, 0, 0)),
                  _const_spec((1, D)), _const_spec((D, 2 * MEM_W))],
        out_specs=[pl.BlockSpec((1, M, MEM_W), lambda b: (b, 0, 0))] * 2,
        out_shape=[jax.ShapeDtypeStruct((B, M, MEM_W), F32)] * 2,
        compiler_params=pltpu.CompilerParams(dimension_semantics=("arbitrary",),
                                             vmem_limit_bytes=VMEM_LIMIT),
        name="mem_kv",
    )(mem, g_mem, w_mem_kv)


def _proj_prompt_tile(t, x_ref, gmix_ref, wqkv_ref, wf_ref, bf_ref, wrnn_ref, wqm_ref,
                      wconv_ref, bconv_ref, wax_ref, bax_ref, lam_ref, mk_ref, mv_ref,
                      k_ref, v_ref, logf_ref, ck_ref, qb_ref, kb_ref, vb_ref,
                      olru_ref, omem_ref, xtail_ref, htail_ref,
                      xext_scr, a_scr, g_scr, h_scr, c_scr):
    tm = x_ref.shape[1]
    pad = a_scr.shape[0] - tm

    @pl.when(t == 0)
    def _():
        xext_scr[0:SUBLANES, :] = jnp.zeros((SUBLANES, LRU_W), F32)
        h_scr[...] = jnp.zeros_like(h_scr)
        c_scr[...] = jnp.zeros_like(c_scr)
        a_scr[0:pad, :] = jnp.ones((pad, LRU_W), F32)
        g_scr[0:pad, :] = jnp.zeros((pad, LRU_W), F32)

    u = _rms(x_ref[0], gmix_ref[...]).astype(BF16)

    zqkv = _dot(u, wqkv_ref[...])
    k_ref[0] = zqkv[:, FOX_W:2 * FOX_W]
    v_ref[0] = zqkv[:, 2 * FOX_W:]
    qb_ref[0] = zqkv[:, :FOX_W].astype(BF16)
    kb_ref[0] = zqkv[:, FOX_W:2 * FOX_W].astype(BF16)
    vb_ref[0] = zqkv[:, 2 * FOX_W:].astype(BF16)

    logf = _log_sigmoid(_dot(u, wf_ref[...]) + bf_ref[...])
    logf_ref[0] = logf[:, :N_HEADS]
    c = _lane_cumsum(logf.T[:N_HEADS, :], tm) + c_scr[:, 0:1]
    ck_ref[0] = c
    c_scr[...] = jnp.broadcast_to(c[:, tm - 1:tm], c_scr.shape)

    zr = _dot(u, wrnn_ref[...])
    xr = zr[:, :LRU_W]
    xext_scr[SUBLANES:SUBLANES + tm, :] = xr
    xtail_ref[0] = xr[tm - SUBLANES:, :]
    xc = bconv_ref[...] + wconv_ref[CONV_LRU - 1:CONV_LRU, :] * xr
    for j in range(CONV_LRU - 1):
        off = SUBLANES - (CONV_LRU - 1) + j
        xc = xc + wconv_ref[j:j + 1, :] * xext_scr[off:off + tm, :]
    xext_scr[0:SUBLANES, :] = xr[tm - SUBLANES:, :]

    a, gated = _lru_coeffs(xc, wax_ref, bax_ref, lam_ref)
    d = 1
    while d < tm:
        a_scr[pad:pad + tm, :] = a
        g_scr[pad:pad + tm, :] = gated
        gated = gated + a * g_scr[pad - d:pad - d + tm, :]
        a = a * a_scr[pad - d:pad - d + tm, :]
        d *= 2
    hs = gated + a * h_scr[0:1, :]
    h_scr[...] = jnp.broadcast_to(hs[tm - 1:tm, :], h_scr.shape)
    htail_ref[0] = hs[tm - SUBLANES:, :]
    olru_ref[0] = (hs * _gelu(zr[:, LRU_W:])).astype(BF16)

    zq = _dot(u, wqm_ref[...]) * (MEM_DIM ** -0.5)
    for h in range(MEM_HEADS):
        sl = slice(h * MEM_DIM, (h + 1) * MEM_DIM)
        o = _mem_attend_head(zq[:, sl].astype(BF16), mk_ref[0, :, sl].astype(BF16),
                             mv_ref[0, :, sl].astype(BF16))
        omem_ref[0, :, sl] = o.astype(BF16)


def _proj_prompt_call_parts(x, mk, mv, w, in_tile_of, out_tile_of):
    B, S, D = x.shape
    M = mk.shape[1]
    tm = TM_PROJ

    def spec(block, place, tile_of=out_tile_of):
        return pl.BlockSpec(block, lambda *g: place(*tile_of(*g)))

    row = lambda width: spec((1, tm, width), lambda b, t: (b, t, 0))
    per_b = lambda rows, width: spec((1, rows, width), lambda b, t: (b, 0, 0))
    consts = [w["g_mix"], w["w_qkv"], w["w_f"], w["b_f"], w["w_rnn"], w["w_qm"],
              w["w_lru_conv"], w["b_lru_conv"], w["w_ax"], w["b_ax"], w["lam"]]
    out_shape = [
        jax.ShapeDtypeStruct((B, S, FOX_W), F32),
        jax.ShapeDtypeStruct((B, S, FOX_W), F32),
        jax.ShapeDtypeStruct((B, S, N_HEADS), F32),
        jax.ShapeDtypeStruct((B, N_HEADS, S), F32),
        jax.ShapeDtypeStruct((B, S, FOX_W), BF16),
        jax.ShapeDtypeStruct((B, S, FOX_W), BF16),
        jax.ShapeDtypeStruct((B, S, FOX_W), BF16),
        jax.ShapeDtypeStruct((B, S, LRU_W), BF16),
        jax.ShapeDtypeStruct((B, S, MEM_W), BF16),
        jax.ShapeDtypeStruct((B, SUBLANES, LRU_W), F32),
        jax.ShapeDtypeStruct((B, SUBLANES, LRU_W), F32),
    ]
    out_specs = [row(FOX_W), row(FOX_W), row(N_HEADS),
                 spec((1, N_HEADS, tm), lambda b, t: (b, 0, t)),
                 row(FOX_W), row(FOX_W), row(FOX_W), row(LRU_W), row(MEM_W),
                 per_b(SUBLANES, LRU_W), per_b(SUBLANES, LRU_W)]
    pad = tm // 2
    in_specs = ([spec((1, tm, D), lambda b, t: (b, t, 0), in_tile_of)]
                + [_const_spec(c.shape) for c in consts]
                + [spec((1, M, MEM_W), lambda b, t: (b, 0, 0), in_tile_of)] * 2)
    scratch_shapes = [pltpu.VMEM((tm + SUBLANES, LRU_W), F32),
                      pltpu.VMEM((tm + pad, LRU_W), F32),
                      pltpu.VMEM((tm + pad, LRU_W), F32),
                      pltpu.VMEM((SUBLANES, LRU_W), F32),
                      pltpu.VMEM((N_HEADS, LANES), F32)]
    return [x, *consts, mk, mv], in_specs, out_specs, out_shape, scratch_shapes


def _fox_prompt_kernel(q_ref, k_ref, v_ref, ck_ref, o_ref, s_scr, p_scr, m_scr, acc_scr):
    tq = q_ref.shape[1]
    i = pl.program_id(2)
    q2 = q_ref[0]
    first = lax.broadcasted_iota(jnp.int32, q2.shape, 1) < HEAD_DIM
    zero = jnp.zeros_like(q2)
    qs = jnp.concatenate([jnp.where(first, q2, zero), jnp.where(first, zero, q2)], axis=0)
    ones = jnp.ones((tq, LANES), BF16)

    def logits(j):
        return _dot_nt(qs, k_ref[0, pl.ds(pl.multiple_of(j * tq, tq), tq), :])

    def weighted_values(p, j):
        v_aug = jnp.concatenate([v_ref[0, pl.ds(pl.multiple_of(j * tq, tq), tq), :], ones], axis=1)
        return _dot(p, v_aug)

    def softmax_block(j, slot, pv, masked):
        start = pl.multiple_of(j * tq, tq)
        if masked:
            causal = (lax.broadcasted_iota(jnp.int32, (tq, tq), 1)
                      <= lax.broadcasted_iota(jnp.int32, (tq, tq), 0))
        for hh in range(2):
            rows = slice(hh * tq, (hh + 1) * tq)
            s = s_scr[slot, rows, :] - ck_ref[0, 0, hh:hh + 1, pl.ds(start, tq)]
            if masked:
                s = jnp.where(causal, s, NEG_BIG)
            m_new = jnp.max(s, axis=-1, keepdims=True)
            if pv is None:
                acc_scr[rows, :] = jnp.zeros((tq, acc_scr.shape[1]), F32)
            else:
                m_old = m_scr[rows, :]
                m_new = jnp.maximum(m_old, m_new)
                acc_scr[rows, :] = jnp.exp(m_old - m_new) * (acc_scr[rows, :] + pv[rows, :])
            p_scr[slot, rows, :] = jnp.exp(s - m_new).astype(BF16)
            m_scr[rows, :] = m_new

    def step(t, slot):
        other = 1 - slot
        pv = weighted_values(p_scr[other], t - 1)
        s_scr[other] = logits(t + 1)
        softmax_block(t, slot, pv, False)

    def finish(slot, pv):
        softmax_block(i, slot, pv, True)
        acc = acc_scr[...] + weighted_values(p_scr[slot], i)
        out = acc[:, :LANES] / acc[:, LANES:]
        o_ref[0] = jnp.where(first, out[:tq, :], out[tq:, :]).astype(o_ref.dtype)

    s_scr[0] = logits(0)

    @pl.when(i == 0)
    def _():
        finish(0, None)

    @pl.when(i > 0)
    def _():
        s_scr[1] = logits(1)
        softmax_block(0, 0, None, False)

        def two_steps(tt, carry):
            step(2 * tt + 1, 1)
            step(2 * tt + 2, 0)
            return carry

        lax.fori_loop(0, (i - 1) // 2, two_steps, 0)

        @pl.when(i % 2 == 0)
        def _():
            step(i - 1, 1)
            finish(0, weighted_values(p_scr[1], i - 1))

        @pl.when(i % 2 == 1)
        def _():
            finish(1, weighted_values(p_scr[0], i - 1))


def _fox_prompt(qb, kb, vb, ck):
    B, S, W = qb.shape
    npair = W // LANES
    ck4 = ck.reshape(B, npair, 2, S)
    return pl.pallas_call(
        _fox_prompt_kernel,
        grid=(B, npair, S // TQ),
        in_specs=[pl.BlockSpec((1, TQ, LANES), lambda b, h, i: (b, i, h)),
                  pl.BlockSpec((1, S, LANES), lambda b, h, i: (b, 0, h)),
                  pl.BlockSpec((1, S, LANES), lambda b, h, i: (b, 0, h)),
                  pl.BlockSpec((1, 1, 2, S), lambda b, h, i: (b, h, 0, 0))],
        out_specs=pl.BlockSpec((1, TQ, LANES), lambda b, h, i: (b, i, h)),
        out_shape=jax.ShapeDtypeStruct((B, S, W), BF16),
        scratch_shapes=[pltpu.VMEM((2, 2 * TQ, TQ), F32), pltpu.VMEM((2, 2 * TQ, TQ), BF16),
                        pltpu.VMEM((2 * TQ, 1), F32), pltpu.VMEM((2 * TQ, 2 * LANES), F32)],
        compiler_params=pltpu.CompilerParams(
            dimension_semantics=("arbitrary", "arbitrary", "arbitrary"),
            vmem_limit_bytes=VMEM_LIMIT),
        name="fox_prompt",
    )(qb, kb, vb, ck4)


def _merge(x, branches, gmix_ref, wgate_ref, bgate_ref, wout_ref):
    D = x.shape[1]
    u = _rms(x, gmix_ref[...]).astype(BF16)
    merged = None
    for j, (o, wo_ref) in enumerate(branches):
        sl = slice(j * D, (j + 1) * D)
        gate = jax.nn.sigmoid(_dot(u, wgate_ref[:, sl]) + bgate_ref[:, sl])
        term = gate * _dot(o, wo_ref[...])
        merged = term if merged is None else merged + term
    return x + _dot(merged.astype(BF16), wout_ref[...])


def _merge_ffn_prompt_kernel(x_ref, ofox_ref, olru_ref, omem_ref, gmix_ref, wgate_ref, bgate_ref,
                             wofox_ref, wolru_ref, womem_ref, wout_ref, gffn_ref, wfg_ref, wfu_ref,
                             wfc_ref, bfc_ref, wfd_ref, gfin_ref,
                             y_ref, gtail_ref, gext_scr):
    tm = x_ref.shape[1]
    dff = wfg_ref.shape[1]
    t = pl.program_id(1)

    @pl.when(t == 0)
    def _():
        gext_scr[0:SUBLANES, :] = jnp.zeros((SUBLANES, dff), F32)

    h = _merge(x_ref[0], [(ofox_ref[0], wofox_ref), (olru_ref[0], wolru_ref),
                          (omem_ref[0], womem_ref)], gmix_ref, wgate_ref, bgate_ref, wout_ref)
    v2 = _rms(h, gffn_ref[...]).astype(BF16)
    y = h
    for c in range(dff // FF_CHUNK):
        sl = slice(c * FF_CHUNK, (c + 1) * FF_CHUNK)
        g = _dot(v2, wfg_ref[:, sl])
        gext_scr[SUBLANES:SUBLANES + tm, sl] = g
        gc = bfc_ref[:, sl] + wfc_ref[CONV_FFN - 1:CONV_FFN, sl] * g
        for j in range(CONV_FFN - 1):
            off = SUBLANES - (CONV_FFN - 1) + j
            gc = gc + wfc_ref[j:j + 1, sl] * gext_scr[off:off + tm, sl]
        act = (_gelu(gc) * _dot(v2, wfu_ref[:, sl])).astype(BF16)
        y = y + _dot(act, wfd_ref[sl, :])
    tail = gext_scr[tm:tm + SUBLANES, :]
    gtail_ref[0] = tail
    gext_scr[0:SUBLANES, :] = tail
    y_ref[0] = _rms(y, gfin_ref[...])


def _merge_ffn_consts(w):
    return [w["g_mix"], w["w_gate"], w["b_gate"], w["w_o_fox"], w["w_o_lru"], w["w_o_mem"],
            w["w_out"], w["g_ffn"], w["w_ffn_gate"], w["w_ffn_up"], w["w_ffn_conv"],
            w["b_ffn_conv"], w["w_ffn_down"], w["g_final"]]


def _merge_ffn_prompt(x, ofox, olru, omem, w):
    B, S, D = x.shape
    dff = w["w_ffn_gate"].shape[1]
    tm = TM_FFN
    row = lambda width: pl.BlockSpec((1, tm, width), lambda b, t: (b, t, 0))
    consts = _merge_ffn_consts(w)
    return pl.pallas_call(
        _merge_ffn_prompt_kernel,
        grid=(B, S // tm),
        in_specs=[row(D), row(FOX_W), row(LRU_W), row(MEM_W)]
                 + [_const_spec(c.shape) for c in consts],
        out_specs=[row(D), pl.BlockSpec((1, SUBLANES, dff), lambda b, t: (b, 0, 0))],
        out_shape=[jax.ShapeDtypeStruct((B, S, D), F32),
                   jax.ShapeDtypeStruct((B, SUBLANES, dff), F32)],
        scratch_shapes=[pltpu.VMEM((tm + SUBLANES, dff), F32)],
        compiler_params=pltpu.CompilerParams(dimension_semantics=("arbitrary", "arbitrary"),
                                             vmem_limit_bytes=VMEM_LIMIT),
        name="merge_ffn_prompt",
    )(x, ofox, olru, omem, *consts)


def _proj_sample_kernel(x_ref, gmix_ref, wqkv_ref, wf_ref, bf_ref, wrnn_ref, wqm_ref,
                        wconv_ref, bconv_ref, wax_ref, bax_ref, lam_ref, cstate_ref, h0_ref,
                        q_ref, k_ref, v_ref, logf_ref, qm_ref, olru_ref, xtail_ref, hlast_ref,
                        x_scr, h_scr):
    nb = h0_ref.shape[0]
    nt = x_ref.shape[0] // nb
    u = _rms(x_ref[...], gmix_ref[...]).astype(BF16)
    zqkv = _dot(u, wqkv_ref[...])
    q_ref[...] = zqkv[:, :FOX_W]
    k_ref[...] = zqkv[:, FOX_W:2 * FOX_W]
    v_ref[...] = zqkv[:, 2 * FOX_W:]
    logf = _log_sigmoid(_dot(u, wf_ref[...]) + bf_ref[...])
    logf_ref[...] = logf[:, :N_HEADS]
    qm_ref[...] = _dot(u, wqm_ref[...]) * (MEM_DIM ** -0.5)

    zr = _dot(u, wrnn_ref[...])
    _park(x_scr, zr[:, :LRU_W])
    xs = [cstate_ref[j] for j in range(CONV_LRU - 1)]
    xs += [_token_rows(x_scr, t, nb, nt) for t in range(nt)]
    h = h0_ref[...]
    for t in range(nt):
        xc = bconv_ref[...]
        for j in range(CONV_LRU):
            xc = xc + wconv_ref[j:j + 1, :] * xs[t + j]
        a, gated = _lru_coeffs(xc, wax_ref, bax_ref, lam_ref)
        h = a * h + gated
        _set_token_rows(h_scr, t, nb, nt, h)
    hlast_ref[...] = h
    for j in range(CONV_LRU - 1):
        xtail_ref[j] = xs[nt + j]
    olru_ref[...] = (_unpark(h_scr) * _gelu(zr[:, LRU_W:])).astype(BF16)


def _proj_sample(x2, cstate_tm, h0, w):
    R, D = x2.shape
    nb = h0.shape[0]
    nt = R // nb
    rb = SAMPLE_ROWS
    sb = rb // nt
    consts = [w["g_mix"], w["w_qkv"], w["w_f"], w["b_f"], w["w_rnn"], w["w_qm"],
              w["w_lru_conv"], w["b_lru_conv"], w["w_ax"], w["b_ax"], w["lam"]]
    row = lambda width: pl.BlockSpec((rb, width), lambda i: (i, 0))
    state = pl.BlockSpec((CONV_LRU - 1, sb, LRU_W), lambda i: (0, i, 0))
    seq = pl.BlockSpec((sb, LRU_W), lambda i: (i, 0))
    out_shape = [
        jax.ShapeDtypeStruct((R, FOX_W), F32),
        jax.ShapeDtypeStruct((R, FOX_W), F32),
        jax.ShapeDtypeStruct((R, FOX_W), F32),
        jax.ShapeDtypeStruct((R, N_HEADS), F32),
        jax.ShapeDtypeStruct((R, MEM_W), F32),
        jax.ShapeDtypeStruct((R, LRU_W), BF16),
        jax.ShapeDtypeStruct((CONV_LRU - 1, nb, LRU_W), F32),
        jax.ShapeDtypeStruct((nb, LRU_W), F32),
    ]
    return pl.pallas_call(
        _proj_sample_kernel,
        grid=(R // rb,),
        in_specs=[row(D)] + [_const_spec(c.shape) for c in consts] + [state, seq],
        out_specs=[row(FOX_W), row(FOX_W), row(FOX_W), row(N_HEADS), row(MEM_W), row(LRU_W),
                   state, seq],
        out_shape=out_shape,
        scratch_shapes=[pltpu.VMEM((LRU_W // LANES, rb, LANES), F32)] * 2,
        compiler_params=pltpu.CompilerParams(dimension_semantics=("arbitrary",),
                                             vmem_limit_bytes=VMEM_LIMIT),
        name="proj_sample",
    )(x2, *consts, cstate_tm, h0)


def _mem_sample_kernel(q_ref, mk_ref, mv_ref, o_ref):
    nb = mk_ref.shape[0]
    nt = q_ref.shape[0] // nb
    nrow = MEM_HEADS * nt
    row_head = lax.broadcasted_iota(jnp.int32, (nrow, mk_ref.shape[1]), 0) // nt
    col_head = lax.broadcasted_iota(jnp.int32, (nrow, mk_ref.shape[1]), 1) % MEM_HEADS
    own = row_head == col_head
    for b in range(nb):
        rows = slice(b * nt, (b + 1) * nt)
        q = jnp.concatenate([q_ref[rows, h * MEM_DIM:(h + 1) * MEM_DIM] for h in range(MEM_HEADS)],
                            axis=0).astype(BF16)
        s = jnp.where(own, _dot_nt(q, mk_ref[b].astype(BF16)), NEG_BIG)
        p = jnp.exp(s - jnp.max(s, axis=-1, keepdims=True))
        o = _dot(p.astype(BF16), mv_ref[b].astype(BF16)) / jnp.sum(p, axis=-1, keepdims=True)
        for h in range(MEM_HEADS):
            o_ref[rows, h * MEM_DIM:(h + 1) * MEM_DIM] = o[h * nt:(h + 1) * nt, :]


def _mem_sample(qm, mk, mv):
    R, W = qm.shape
    DB, MH, Dm = mk.shape
    nt = R // DB
    nb = MEM_BATCH
    kv_spec = pl.BlockSpec((nb, MH, Dm), lambda i: (i, 0, 0))
    return pl.pallas_call(
        _mem_sample_kernel,
        grid=(DB // nb,),
        in_specs=[pl.BlockSpec((nb * nt, W), lambda i: (i, 0)), kv_spec, kv_spec],
        out_specs=pl.BlockSpec((nb * nt, W), lambda i: (i, 0)),
        out_shape=jax.ShapeDtypeStruct((R, W), F32),
        compiler_params=pltpu.CompilerParams(dimension_semantics=("arbitrary",),
                                             vmem_limit_bytes=VMEM_LIMIT),
        name="mem_sample",
    )(qm, mk, mv)


def _fox_sample_kernel(pt_ref, q_ref, kn_ref, vn_ref, lfn_ref, k_hbm, v_hbm, f_hbm, o_ref,
                       k_buf, v_buf, f_buf, k_sem, v_sem, f_sem,
                       kb_scr, vb_scr, qbd_scr, m_scr, l_scr, acc_scr, c_scr, pack_scr):
    npg = PAGES_PER_STEP
    nslot = k_buf.shape[0]
    nt = q_ref.shape[0]
    nrow = N_HEADS * nt
    c = pl.program_id(1)
    nch = pl.num_programs(1)
    step = pl.program_id(0) * nch + c
    last = pl.num_programs(0) * nch - 1
    lane = lax.broadcasted_iota(jnp.int32, (nt, FOX_W), 1)
    streams = ((k_hbm, k_buf, k_sem), (v_hbm, v_buf, v_sem), (f_hbm, f_buf, f_sem))

    def page_copies(slot, pages):
        return [pltpu.make_async_copy(hbm.at[pages[p]], buf.at[slot, p], sem.at[slot])
                for hbm, buf, sem in streams for p in range(npg)]

    def fetch(s):
        src = jnp.minimum(s, last)
        row = src // nch
        col = (src % nch) * npg
        for cp in page_copies(s % nslot, [pt_ref[row, col + p] for p in range(npg)]):
            cp.start(priority=PAGE_DMA_PRIORITY)

    def wait(slot):
        for cp in page_copies(slot, [0] * npg):
            cp.wait()

    @pl.when(step == 0)
    def _():
        for s in range(nslot - 1):
            fetch(jnp.int32(s))

    slot = step % nslot
    wait(slot)
    fetch(step + (nslot - 1))
    k_pages = [k_buf.at[slot, p] for p in range(npg)]
    v_pages = [v_buf.at[slot, p] for p in range(npg)]
    f_pages = [f_buf.at[slot, p] for p in range(npg)]

    @pl.when(c == 0)
    def _():
        q = q_ref[...]
        for h in range(N_HEADS):
            mine = (lane >= h * HEAD_DIM) & (lane < (h + 1) * HEAD_DIM)
            qbd_scr[h * nt:(h + 1) * nt, :] = jnp.where(mine, q, 0.0).astype(BF16)
        m_scr[...] = jnp.full_like(m_scr, NEG_BIG)
        l_scr[...] = jnp.zeros_like(l_scr)
        acc_scr[...] = jnp.zeros_like(acc_scr)
        c_scr[...] = jnp.zeros_like(c_scr)

    def update(s, pv):
        m = m_scr[...]
        m_new = jnp.maximum(m, jnp.max(s, axis=-1, keepdims=True))
        alpha = jnp.exp(m - m_new)
        p = jnp.exp(s - m_new)
        l_scr[...] = alpha * l_scr[...] + jnp.sum(p, axis=-1, keepdims=True)
        acc_scr[...] = alpha * acc_scr[...] + pv(p.astype(BF16))
        m_scr[...] = m_new

    def head_rows(cb):
        return jnp.concatenate(
            [jnp.broadcast_to(cb[h:h + 1, :], (nt, cb.shape[1])) for h in range(N_HEADS)], axis=0)

    z = _lane_cumsum(jnp.concatenate([f_pages[p][...] for p in range(npg)], axis=0), PAGE)
    tiles = [z[p * N_HEADS:(p + 1) * N_HEADS, :] for p in range(npg)]
    totals = [jnp.broadcast_to(t[:, PAGE - 1:PAGE], t.shape) for t in tiles]
    off = c_scr[...]
    for p in range(npg):
        tiles[p] = tiles[p] + off
        off = off + totals[p]
    c_scr[...] = off

    for p in range(npg):
        kb_scr[:, p * PAGE:(p + 1) * PAGE] = k_pages[p][...].astype(BF16)
        vb_scr[:, p * PAGE:(p + 1) * PAGE] = v_pages[p][...].astype(BF16)
    s = _dot(qbd_scr[...], kb_scr[...]) - head_rows(jnp.concatenate(tiles, axis=1))
    update(s, lambda p: _dot_nt(p, vb_scr[...]))

    @pl.when(c == pl.num_programs(1) - 1)
    def _():
        zeros = jnp.zeros((PAGE - nt, FOX_W), F32)
        kn = jnp.concatenate([kn_ref[...], zeros], axis=0).astype(BF16)
        vn = jnp.concatenate([vn_ref[...], zeros], axis=0).astype(BF16)
        pack_scr[...] = jnp.zeros_like(pack_scr)
        pack_scr[0:nt, 0:N_HEADS] = lfn_ref[0]
        cn = _lane_cumsum(pack_scr[...].T[:N_HEADS, :], PAGE) + c_scr[...]
        s = _dot_nt(qbd_scr[...], kn) - head_rows(cn)
        tok = lax.broadcasted_iota(jnp.int32, (nrow, PAGE), 0) % nt
        key = lax.broadcasted_iota(jnp.int32, (nrow, PAGE), 1)
        update(jnp.where(key <= tok, s, NEG_BIG), lambda p: _dot(p, vn))
        acc = acc_scr[...] / l_scr[...]
        out = acc[0:nt, :]
        for h in range(1, N_HEADS):
            out = jnp.where(lane >= h * HEAD_DIM, acc[h * nt:(h + 1) * nt, :], out)
        o_ref[...] = out

    @pl.when(step == last)
    def _():
        for ahead in range(1, nslot):
            wait((step + ahead) % nslot)


N_FOX_SAMPLE_IN = 7
N_FOX_SAMPLE_SCRATCH = 14
N_PROJ_PROMPT_IN = 14
N_PROJ_PROMPT_OUT = 11


def _fox_sample_proj_prompt_kernel(steps_per_tile, tiles_per_seq, pt_ref, *refs):
    bounds = [N_FOX_SAMPLE_IN, N_PROJ_PROMPT_IN, 1, N_PROJ_PROMPT_OUT, N_FOX_SAMPLE_SCRATCH]
    parts, at = [], 0
    for n in bounds:
        parts.append(refs[at:at + n])
        at += n
    fox_in, proj_in, fox_out, proj_out, fox_scratch = parts
    proj_scratch = refs[at:]
    _fox_sample_kernel(pt_ref, *fox_in, *fox_out, *fox_scratch)
    step = pl.program_id(0) * pl.num_programs(1) + pl.program_id(1)

    @pl.when(step % steps_per_tile == 0)
    def _():
        _proj_prompt_tile((step // steps_per_tile) % tiles_per_seq,
                          *proj_in, *proj_out, *proj_scratch)


def _fox_sample_proj_prompt(page_table, q, k_new, v_new, logf_new, cache_kt, cache_vt, cache_ft,
                            x, mk, mv, w):
    DB, npages = page_table.shape
    R, W = q.shape
    nt = R // DB
    npg = PAGES_PER_STEP
    nk = npg * PAGE
    nch = npages // npg
    nslot = FETCH_SLOTS
    tiles_per_seq = x.shape[1] // TM_PROJ
    steps_per_tile = (DB * nch) // (x.shape[0] * tiles_per_seq)
    assert steps_per_tile * x.shape[0] * tiles_per_seq == DB * nch

    last_tile = x.shape[0] * tiles_per_seq - 1

    def tile_of(b, c, pt, ahead=0):
        tile = jnp.minimum((b * nch + c + ahead) // steps_per_tile, last_tile)
        return tile // tiles_per_seq, tile % tiles_per_seq

    proj_args, proj_in_specs, proj_out_specs, proj_out_shape, proj_scratch = (
        _proj_prompt_call_parts(x, mk, mv, w, functools.partial(tile_of, ahead=nslot - 1), tile_of))
    assert (len(proj_args), len(proj_out_specs)) == (N_PROJ_PROMPT_IN, N_PROJ_PROMPT_OUT)
    tok = pl.BlockSpec((nt, W), lambda b, c, pt: (b, 0))
    hbm = pl.BlockSpec(memory_space=pl.ANY)
    fox_in_specs = [tok, tok, tok, pl.BlockSpec((1, nt, N_HEADS), lambda b, c, pt: (b, 0, 0)),
                    hbm, hbm, hbm]
    fox_scratch = [pltpu.VMEM((nslot, npg, W, PAGE), F32),
                   pltpu.VMEM((nslot, npg, W, PAGE), F32),
                   pltpu.VMEM((nslot, npg, N_HEADS, PAGE), F32),
                   pltpu.SemaphoreType.DMA((nslot,)), pltpu.SemaphoreType.DMA((nslot,)),
                   pltpu.SemaphoreType.DMA((nslot,)),
                   pltpu.VMEM((W, nk), BF16), pltpu.VMEM((W, nk), BF16),
                   pltpu.VMEM((N_HEADS * nt, W), BF16),
                   pltpu.VMEM((N_HEADS * nt, 1), F32), pltpu.VMEM((N_HEADS * nt, 1), F32),
                   pltpu.VMEM((N_HEADS * nt, W), F32),
                   pltpu.VMEM((N_HEADS, PAGE), F32), pltpu.VMEM((PAGE, LANES), F32)]
    assert (len(fox_in_specs), len(fox_scratch)) == (N_FOX_SAMPLE_IN, N_FOX_SAMPLE_SCRATCH)
    grid_spec = pltpu.PrefetchScalarGridSpec(
        num_scalar_prefetch=1,
        grid=(DB, nch),
        in_specs=fox_in_specs + proj_in_specs,
        out_specs=[tok] + proj_out_specs,
        scratch_shapes=fox_scratch + proj_scratch,
    )
    return pl.pallas_call(
        functools.partial(_fox_sample_proj_prompt_kernel, steps_per_tile, tiles_per_seq),
        grid_spec=grid_spec,
        out_shape=[jax.ShapeDtypeStruct((R, W), F32)] + proj_out_shape,
        compiler_params=pltpu.CompilerParams(dimension_semantics=("arbitrary", "arbitrary"),
                                             vmem_limit_bytes=VMEM_LIMIT),
        name="fox_sample_proj_prompt",
    )(page_table, q, k_new, v_new, logf_new, cache_kt, cache_vt, cache_ft, *proj_args)


def _merge_ffn_sample_kernel(x_ref, ofox_ref, olru_ref, omem_ref, fstate_ref, gmix_ref, wgate_ref,
                             bgate_ref, wofox_ref, wolru_ref, womem_ref, wout_ref, gffn_ref,
                             wfg_ref, wfu_ref, wfc_ref, bfc_ref, wfd_ref, gfin_ref,
                             y_ref, gtail_ref, g_scr, gc_scr):
    nb = fstate_ref.shape[1]
    nt = x_ref.shape[0] // nb
    dff = wfg_ref.shape[1]
    h = _merge(x_ref[...], [(ofox_ref[...].astype(BF16), wofox_ref), (olru_ref[...], wolru_ref),
                            (omem_ref[...].astype(BF16), womem_ref)],
               gmix_ref, wgate_ref, bgate_ref, wout_ref)
    v2 = _rms(h, gffn_ref[...]).astype(BF16)
    y = h
    for c in range(dff // FF_CHUNK):
        sl = slice(c * FF_CHUNK, (c + 1) * FF_CHUNK)
        _park(g_scr, _dot(v2, wfg_ref[:, sl]))
        gs = [fstate_ref[j, :, sl] for j in range(CONV_FFN - 1)]
        gs += [_token_rows(g_scr, t, nb, nt) for t in range(nt)]
        for t in range(nt):
            gc = bfc_ref[:, sl]
            for j in range(CONV_FFN):
                gc = gc + wfc_ref[j:j + 1, sl] * gs[t + j]
            _set_token_rows(gc_scr, t, nb, nt, gc)
        for j in range(CONV_FFN - 1):
            gtail_ref[j, :, sl] = gs[nt + j]
        act = (_gelu(_unpark(gc_scr)) * _dot(v2, wfu_ref[:, sl])).astype(BF16)
        y = y + _dot(act, wfd_ref[sl, :])
    y_ref[...] = _rms(y, gfin_ref[...])


def _merge_ffn_sample(x2, ofox, olru, omem, fstate_tm, w):
    R, D = x2.shape
    nb = fstate_tm.shape[1]
    nt = R // nb
    dff = w["w_ffn_gate"].shape[1]
    rb = SAMPLE_ROWS
    sb = rb // nt
    consts = _merge_ffn_consts(w)
    row = lambda width: pl.BlockSpec((rb, width), lambda i: (i, 0))
    state = pl.BlockSpec((CONV_FFN - 1, sb, dff), lambda i: (0, i, 0))
    return pl.pallas_call(
        _merge_ffn_sample_kernel,
        grid=(R // rb,),
        in_specs=[row(D), row(FOX_W), row(LRU_W), row(MEM_W), state]
                 + [_const_spec(c.shape) for c in consts],
        out_specs=[row(D), state],
        out_shape=[jax.ShapeDtypeStruct((R, D), F32),
                   jax.ShapeDtypeStruct((CONV_FFN - 1, nb, dff), F32)],
        scratch_shapes=[pltpu.VMEM((FF_CHUNK // LANES, rb, LANES), F32)] * 2,
        compiler_params=pltpu.CompilerParams(dimension_semantics=("arbitrary",),
                                             vmem_limit_bytes=VMEM_LIMIT),
        name="merge_ffn_sample",
    )(x2, ofox, olru, omem, fstate_tm, *consts)


def _block_diag(wb):
    nb, bs, _ = wb.shape
    eye = jnp.eye(nb, dtype=wb.dtype)
    return (eye[:, None, :, None] * wb[:, :, None, :]).reshape(nb * bs, nb * bs)


def _prep_weights(l, g_mix, w_in, b_f, w_o_fox, w_lru_conv, b_lru_conv, w_lru_a, b_lru_a, w_lru_x,
                  b_lru_x, lru_lambda, w_o_lru, w_o_mem, w_gate, b_gate, w_out, g_ffn, w_ffn_gate,
                  w_ffn_up, w_ffn_conv, b_ffn_conv, w_ffn_down, g_final):
    wi = w_in[l]
    o_f = 3 * FOX_W
    o_r = o_f + N_HEADS
    o_m = o_r + 2 * LRU_W
    row = lambda v: v.reshape(1, -1)
    w_qkv = jnp.concatenate([wi[:, :FOX_W] * (HEAD_DIM ** -0.5), wi[:, FOX_W:o_f]], axis=1)
    return {
        "g_mix": row(g_mix[l]),
        "w_qkv": w_qkv.astype(BF16),
        "w_f": jnp.pad(wi[:, o_f:o_r], ((0, 0), (0, LANES - N_HEADS))).astype(BF16),
        "b_f": jnp.pad(row(b_f[l]), ((0, 0), (0, LANES - N_HEADS))),
        "w_rnn": wi[:, o_r:o_m].astype(BF16),
        "w_qm": wi[:, o_m:].astype(BF16),
        "w_lru_conv": w_lru_conv[l],
        "b_lru_conv": row(b_lru_conv[l]),
        "w_ax": jnp.concatenate([_block_diag(w_lru_a[l]), _block_diag(w_lru_x[l])],
                                axis=1).astype(BF16),
        "b_ax": jnp.concatenate([row(b_lru_a[l]), row(b_lru_x[l])], axis=1),
        "lam": row(lru_lambda[l]),
        "w_gate": w_gate[l].astype(BF16),
        "b_gate": row(b_gate[l]),
        "w_o_fox": w_o_fox[l].astype(BF16),
        "w_o_lru": w_o_lru[l].astype(BF16),
        "w_o_mem": w_o_mem[l].astype(BF16),
        "w_out": w_out[l].astype(BF16),
        "g_ffn": row(g_ffn[l]),
        "w_ffn_gate": w_ffn_gate[l].astype(BF16),
        "w_ffn_up": w_ffn_up[l].astype(BF16),
        "w_ffn_conv": w_ffn_conv[l],
        "b_ffn_conv": row(b_ffn_conv[l]),
        "w_ffn_down": w_ffn_down[l].astype(BF16),
        "g_final": row(g_final),
    }


def kernel(x_prompt, x_sample, mem_prompt, cache_k, cache_v, cache_logf, cache_mem_k, cache_mem_v,
           state_lru_h, state_lru_conv, state_ffn_conv, page_table,
           g_mix, w_in, b_f, w_o_fox, w_lru_conv, b_lru_conv, w_lru_a, b_lru_a, w_lru_x, b_lru_x,
           lru_lambda, w_o_lru, g_mem, w_mem_kv, w_o_mem, w_gate, b_gate, w_out,
           g_ffn, w_ffn_gate, w_ffn_up, w_ffn_conv, b_ffn_conv, w_ffn_down, g_final):
    depth = w_in.shape[0]
    assert depth == 1, "the final norm is fused into the single layer"
    l = 0
    B, S, D = x_prompt.shape
    DB, T, _ = x_sample.shape
    n_pool = cache_k.shape[1]
    w = _prep_weights(l, g_mix, w_in, b_f, w_o_fox, w_lru_conv, b_lru_conv, w_lru_a, b_lru_a,
                      w_lru_x, b_lru_x, lru_lambda, w_o_lru, w_o_mem, w_gate, b_gate, w_out,
                      g_ffn, w_ffn_gate, w_ffn_up, w_ffn_conv, b_ffn_conv, w_ffn_down, g_final)

    mk_p, mv_p = _mem_kv(mem_prompt, g_mem[l].reshape(1, -1), w_mem_kv[l].astype(BF16))

    x2 = x_sample.reshape(DB * T, D)
    cstate_tm = jnp.swapaxes(state_lru_conv[l], 0, 1)
    fstate_tm = jnp.swapaxes(state_ffn_conv[l], 0, 1)
    (q_s, k_s, v_s, logf_s, qm_s, olru_s, xtail_s, hlast_s) = _proj_sample(
        x2, cstate_tm, state_lru_h[l], w)
    omem_s = _mem_sample(qm_s, cache_mem_k[l].reshape(DB, -1, MEM_DIM),
                         cache_mem_v[l].reshape(DB, -1, MEM_DIM))
    cache_kt = jnp.transpose(cache_k[l], (0, 2, 3, 1)).reshape(n_pool, FOX_W, PAGE)
    cache_vt = jnp.transpose(cache_v[l], (0, 2, 3, 1)).reshape(n_pool, FOX_W, PAGE)
    cache_ft = jnp.transpose(cache_logf[l], (0, 2, 1))
    (ofox_s, k_p, v_p, logf_p, ck_p, qb, kb, vb, olru_p, omem_p, xtail_p, htail_p) = (
        _fox_sample_proj_prompt(page_table, q_s, k_s, v_s, logf_s.reshape(DB, T, N_HEADS),
                                cache_kt, cache_vt, cache_ft, x_prompt, mk_p, mv_p, w))
    y_s, gtail_s = _merge_ffn_sample(x2, ofox_s, olru_s, omem_s, fstate_tm, w)

    ofox_p = _fox_prompt(qb, kb, vb, ck_p)
    y_p, gtail_p = _merge_ffn_prompt(x_prompt, ofox_p, olru_p, omem_p, w)

    heads = lambda a, n: a.reshape(1, n, -1, N_HEADS, HEAD_DIM)
    mem_heads = lambda a: a.reshape(1, B, -1, MEM_HEADS, MEM_DIM)
    return (
        y_p,
        y_s.reshape(DB, T, D),
        heads(k_p, B), heads(v_p, B), logf_p[None],
        mem_heads(mk_p), mem_heads(mv_p),
        htail_p[None, :, -1, :],
        xtail_p[None, :, SUBLANES - (CONV_LRU - 1):, :],
        gtail_p[None, :, SUBLANES - (CONV_FFN - 1):, :],
        heads(k_s, DB), heads(v_s, DB), logf_s.reshape(1, DB, T, N_HEADS),
        hlast_s[None],
        jnp.swapaxes(xtail_s, 0, 1)[None],
        jnp.swapaxes(gtail_s, 0, 1)[None],
    )
```

```python
import functools

import jax
import jax.numpy as jnp
from jax import lax
from jax.experimental import pallas as pl
from jax.experimental.pallas import tpu as pltpu

F32 = jnp.float32
BF16 = jnp.bfloat16

EPS = 1e-6
LRU_C = 8.0
NEG_BIG = -1e30
LANES = 128
SUBLANES = 8
VMEM_LIMIT = 56 * 1024 * 1024

N_HEADS = 8
HEAD_DIM = 64
FOX_W = N_HEADS * HEAD_DIM
LRU_W = 512
MEM_HEADS = 4
MEM_DIM = 128
MEM_W = MEM_HEADS * MEM_DIM
CONV_LRU = 4
CONV_FFN = 3
PAGE = 128

TM_PROJ = 256
TM_FFN = 256
TQ = 512
FF_CHUNK = 1024
PAGES_PER_STEP = 16
FETCH_SLOTS = 3
PAGE_DMA_PRIORITY = 1
MEM_BATCH = 8
SAMPLE_ROWS = 256


def _rms(x, g):
    return x * lax.rsqrt(jnp.mean(x * x, axis=-1, keepdims=True) + EPS) * g


def _log_sigmoid(x):
    return jnp.minimum(x, 0.0) - jnp.log1p(jnp.exp(-jnp.abs(x)))


def _gelu(x):
    return 0.5 * x * (1.0 + jnp.tanh(0.7978845608028654 * (x + 0.044715 * (x * x * x))))


def _dot(a, b):
    return jnp.dot(a, b, preferred_element_type=F32)


def _dot_nt(a, b):
    return lax.dot_general(a, b, (((1,), (1,)), ((), ())), preferred_element_type=F32)


def _lru_coeffs(xc, wax_ref, bax_ref, lam_ref):
    gates = _dot(xc.astype(BF16), wax_ref[...]) + bax_ref[...]
    r = jax.nn.sigmoid(gates[:, :LRU_W])
    i = jax.nn.sigmoid(gates[:, LRU_W:])
    log_a = LRU_C * r * _log_sigmoid(lam_ref[...])
    a = jnp.exp(log_a)
    th = jnp.tanh(log_a)
    gated = jnp.sqrt(-2.0 * th / (1.0 - th)) * i * xc
    return a, gated


def _mem_attend_head(q, mk, mv):
    s = _dot_nt(q, mk)
    m = jnp.max(s, axis=-1, keepdims=True)
    p = jnp.exp(s - m)
    l = jnp.sum(p, axis=-1, keepdims=True)
    return _dot(p.astype(BF16), mv) / l


def _lane_cumsum(x, width):
    lane = lax.broadcasted_iota(jnp.int32, x.shape, x.ndim - 1)
    d = 1
    while d < width:
        x = x + jnp.where(lane >= d, pltpu.roll(x, d, axis=x.ndim - 1), 0.0)
        d *= 2
    return x


def _park(scr, x):
    for c in range(scr.shape[0]):
        scr[c] = x[:, c * LANES:(c + 1) * LANES]


def _unpark(scr):
    return jnp.concatenate([scr[c] for c in range(scr.shape[0])], axis=1)


def _token_rows(scr, t, nseq, ntok):
    return jnp.concatenate([scr[c, pl.ds(t, nseq, stride=ntok), :] for c in range(scr.shape[0])],
                           axis=1)


def _set_token_rows(scr, t, nseq, ntok, x):
    for c in range(scr.shape[0]):
        scr[c, pl.ds(t, nseq, stride=ntok), :] = x[:, c * LANES:(c + 1) * LANES]


def _const_spec(shape):
    nd = len(shape)
    return pl.BlockSpec(shape, lambda *_: (0,) * nd, pipeline_mode=pl.Buffered(1))


def _mem_kv_kernel(mem_ref, g_ref, w_ref, mk_ref, mv_ref):
    u = _rms(mem_ref[0], g_ref[...]).astype(BF16)
    z = _dot(u, w_ref[...])
    mk_ref[0] = z[:, :MEM_W]
    mv_ref[0] = z[:, MEM_W:]


def _mem_kv(mem, g_mem, w_mem_kv):
    B, M, D = mem.shape
    return pl.pallas_call(
        _mem_kv_kernel,
        grid=(B,),
        in_specs=[pl.BlockSpec((1, M, D), lambda b: (b, 0, 0)),
                  _const_spec((1, D)), _const_spec((D, 2 * MEM_W))],
        out_specs=[pl.BlockSpec((1, M, MEM_W), lambda b: (b, 0, 0))] * 2,
        out_shape=[jax.ShapeDtypeStruct((B, M, MEM_W), F32)] * 2,
        compiler_params=pltpu.CompilerParams(dimension_semantics=("arbitrary",),
                                             vmem_limit_bytes=VMEM_LIMIT),
        name="mem_kv",
    )(mem, g_mem, w_mem_kv)


def _proj_prompt_tile(t, x_ref, gmix_ref, wqkv_ref, wf_ref, bf_ref, wrnn_ref, wqm_ref,
                      wconv_ref, bconv_ref, wax_ref, bax_ref, lam_ref, mk_ref, mv_ref,
                      k_ref, v_ref, logf_ref, ck_ref, qb_ref, kb_ref, vb_ref,
                      olru_ref, omem_ref, xtail_ref, htail_ref,
                      xext_scr, a_scr, g_scr, h_scr, c_scr):
    tm = x_ref.shape[1]
    pad = a_scr.shape[0] - tm

    @pl.when(t == 0)
    def _():
        xext_scr[0:SUBLANES, :] = jnp.zeros((SUBLANES, LRU_W), F32)
        h_scr[...] = jnp.zeros_like(h_scr)
        c_scr[...] = jnp.zeros_like(c_scr)
        a_scr[0:pad, :] = jnp.ones((pad, LRU_W), F32)
        g_scr[0:pad, :] = jnp.zeros((pad, LRU_W), F32)

    u = _rms(x_ref[0], gmix_ref[...]).astype(BF16)

    zqkv = _dot(u, wqkv_ref[...])
    k_ref[0] = zqkv[:, FOX_W:2 * FOX_W]
    v_ref[0] = zqkv[:, 2 * FOX_W:]
    qb_ref[0] = zqkv[:, :FOX_W].astype(BF16)
    kb_ref[0] = zqkv[:, FOX_W:2 * FOX_W].astype(BF16)
    vb_ref[0] = zqkv[:, 2 * FOX_W:].astype(BF16)

    logf = _log_sigmoid(_dot(u, wf_ref[...]) + bf_ref[...])
    logf_ref[0] = logf[:, :N_HEADS]
    c = _lane_cumsum(logf.T[:N_HEADS, :], tm) + c_scr[:, 0:1]
    ck_ref[0] = c
    c_scr[...] = jnp.broadcast_to(c[:, tm - 1:tm], c_scr.shape)

    zr = _dot(u, wrnn_ref[...])
    xr = zr[:, :LRU_W]
    xext_scr[SUBLANES:SUBLANES + tm, :] = xr
    xtail_ref[0] = xr[tm - SUBLANES:, :]
    xc = bconv_ref[...] + wconv_ref[CONV_LRU - 1:CONV_LRU, :] * xr
    for j in range(CONV_LRU - 1):
        off = SUBLANES - (CONV_LRU - 1) + j
        xc = xc + wconv_ref[j:j + 1, :] * xext_scr[off:off + tm, :]
    xext_scr[0:SUBLANES, :] = xr[tm - SUBLANES:, :]

    a, gated = _lru_coeffs(xc, wax_ref, bax_ref, lam_ref)
    d = 1
    while d < tm:
        a_scr[pad:pad + tm, :] = a
        g_scr[pad:pad + tm, :] = gated
        gated = gated + a * g_scr[pad - d:pad - d + tm, :]
        a = a * a_scr[pad - d:pad - d + tm, :]
        d *= 2
    hs = gated + a * h_scr[0:1, :]
    h_scr[...] = jnp.broadcast_to(hs[tm - 1:tm, :], h_scr.shape)
    htail_ref[0] = hs[tm - SUBLANES:, :]
    olru_ref[0] = (hs * _gelu(zr[:, LRU_W:])).astype(BF16)

    zq = _dot(u, wqm_ref[...]) * (MEM_DIM ** -0.5)
    for h in range(MEM_HEADS):
        sl = slice(h * MEM_DIM, (h + 1) * MEM_DIM)
        o = _mem_attend_head(zq[:, sl].astype(BF16), mk_ref[0, :, sl].astype(BF16),
                             mv_ref[0, :, sl].astype(BF16))
        omem_ref[0, :, sl] = o.astype(BF16)


def _proj_prompt_call_parts(x, mk, mv, w, in_tile_of, out_tile_of):
    B, S, D = x.shape
    M = mk.shape[1]
    tm = TM_PROJ

    def spec(block, place, tile_of=out_tile_of):
        return pl.BlockSpec(block, lambda *g: place(*tile_of(*g)))

    row = lambda width: spec((1, tm, width), lambda b, t: (b, t, 0))
    per_b = lambda rows, width: spec((1, rows, width), lambda b, t: (b, 0, 0))
    consts = [w["g_mix"], w["w_qkv"], w["w_f"], w["b_f"], w["w_rnn"], w["w_qm"],
              w["w_lru_conv"], w["b_lru_conv"], w["w_ax"], w["b_ax"], w["lam"]]
    out_shape = [
        jax.ShapeDtypeStruct((B, S, FOX_W), F32),
        jax.ShapeDtypeStruct((B, S, FOX_W), F32),
        jax.ShapeDtypeStruct((B, S, N_HEADS), F32),
        jax.ShapeDtypeStruct((B, N_HEADS, S), F32),
        jax.ShapeDtypeStruct((B, S, FOX_W), BF16),
        jax.ShapeDtypeStruct((B, S, FOX_W), BF16),
        jax.ShapeDtypeStruct((B, S, FOX_W), BF16),
        jax.ShapeDtypeStruct((B, S, LRU_W), BF16),
        jax.ShapeDtypeStruct((B, S, MEM_W), BF16),
        jax.ShapeDtypeStruct((B, SUBLANES, LRU_W), F32),
        jax.ShapeDtypeStruct((B, SUBLANES, LRU_W), F32),
    ]
    out_specs = [row(FOX_W), row(FOX_W), row(N_HEADS),
                 spec((1, N_HEADS, tm), lambda b, t: (b, 0, t)),
                 row(FOX_W), row(FOX_W), row(FOX_W), row(LRU_W), row(MEM_W),
                 per_b(SUBLANES, LRU_W), per_b(SUBLANES, LRU_W)]
    pad = tm // 2
    in_specs = ([spec((1, tm, D), lambda b, t: (b, t, 0), in_tile_of)]
                + [_const_spec(c.shape) for c in consts]
                + [spec((1, M, MEM_W), lambda b, t: (b, 0, 0), in_tile_of)] * 2)
    scratch_shapes = [pltpu.VMEM((tm + SUBLANES, LRU_W), F32),
                      pltpu.VMEM((tm + pad, LRU_W), F32),
                      pltpu.VMEM((tm + pad, LRU_W), F32),
                      pltpu.VMEM((SUBLANES, LRU_W), F32),
                      pltpu.VMEM((N_HEADS, LANES), F32)]
    return [x, *consts, mk, mv], in_specs, out_specs, out_shape, scratch_shapes


def _fox_prompt_kernel(q_ref, k_ref, v_ref, ck_ref, o_ref, s_scr, p_scr, m_scr, acc_scr):
    tq = q_ref.shape[1]
    i = pl.program_id(2)
    q2 = q_ref[0]
    first = lax.broadcasted_iota(jnp.int32, q2.shape, 1) < HEAD_DIM
    zero = jnp.zeros_like(q2)
    qs = jnp.concatenate([jnp.where(first, q2, zero), jnp.where(first, zero, q2)], axis=0)
    ones = jnp.ones((tq, LANES), BF16)

    def logits(j):
        return _dot_nt(qs, k_ref[0, pl.ds(pl.multiple_of(j * tq, tq), tq), :])

    def weighted_values(p, j):
        v_aug = jnp.concatenate([v_ref[0, pl.ds(pl.multiple_of(j * tq, tq), tq), :], ones], axis=1)
        return _dot(p, v_aug)

    def softmax_block(j, slot, pv, masked):
        start = pl.multiple_of(j * tq, tq)
        if masked:
            causal = (lax.broadcasted_iota(jnp.int32, (tq, tq), 1)
                      <= lax.broadcasted_iota(jnp.int32, (tq, tq), 0))
        for hh in range(2):
            rows = slice(hh * tq, (hh + 1) * tq)
            s = s_scr[slot, rows, :] - ck_ref[0, 0, hh:hh + 1, pl.ds(start, tq)]
            if masked:
                s = jnp.where(causal, s, NEG_BIG)
            m_new = jnp.max(s, axis=-1, keepdims=True)
            if pv is None:
                acc_scr[rows, :] = jnp.zeros((tq, acc_scr.shape[1]), F32)
            else:
                m_old = m_scr[rows, :]
                m_new = jnp.maximum(m_old, m_new)
                acc_scr[rows, :] = jnp.exp(m_old - m_new) * (acc_scr[rows, :] + pv[rows, :])
            p_scr[slot, rows, :] = jnp.exp(s - m_new).astype(BF16)
            m_scr[rows, :] = m_new

    def step(t, slot):
        other = 1 - slot
        pv = weighted_values(p_scr[other], t - 1)
        s_scr[other] = logits(t + 1)
        softmax_block(t, slot, pv, False)

    def finish(slot, pv):
        softmax_block(i, slot, pv, True)
        acc = acc_scr[...] + weighted_values(p_scr[slot], i)
        out = acc[:, :LANES] / acc[:, LANES:]
        o_ref[0] = jnp.where(first, out[:tq, :], out[tq:, :]).astype(o_ref.dtype)

    s_scr[0] = logits(0)

    @pl.when(i == 0)
    def _():
        finish(0, None)

    @pl.when(i > 0)
    def _():
        s_scr[1] = logits(1)
        softmax_block(0, 0, None, False)

        def two_steps(tt, carry):
            step(2 * tt + 1, 1)
            step(2 * tt + 2, 0)
            return carry

        lax.fori_loop(0, (i - 1) // 2, two_steps, 0)

        @pl.when(i % 2 == 0)
        def _():
            step(i - 1, 1)
            finish(0, weighted_values(p_scr[1], i - 1))

        @pl.when(i % 2 == 1)
        def _():
            finish(1, weighted_values(p_scr[0], i - 1))


def _fox_prompt(qb, kb, vb, ck):
    B, S, W = qb.shape
    npair = W // LANES
    ck4 = ck.reshape(B, npair, 2, S)
    return pl.pallas_call(
        _fox_prompt_kernel,
        grid=(B, npair, S // TQ),
        in_specs=[pl.BlockSpec((1, TQ, LANES), lambda b, h, i: (b, i, h)),
                  pl.BlockSpec((1, S, LANES), lambda b, h, i: (b, 0, h)),
                  pl.BlockSpec((1, S, LANES), lambda b, h, i: (b, 0, h)),
                  pl.BlockSpec((1, 1, 2, S), lambda b, h, i: (b, h, 0, 0))],
        out_specs=pl.BlockSpec((1, TQ, LANES), lambda b, h, i: (b, i, h)),
        out_shape=jax.ShapeDtypeStruct((B, S, W), BF16),
        scratch_shapes=[pltpu.VMEM((2, 2 * TQ, TQ), F32), pltpu.VMEM((2, 2 * TQ, TQ), BF16),
                        pltpu.VMEM((2 * TQ, 1), F32), pltpu.VMEM((2 * TQ, 2 * LANES), F32)],
        compiler_params=pltpu.CompilerParams(
            dimension_semantics=("arbitrary", "arbitrary", "arbitrary"),
            vmem_limit_bytes=VMEM_LIMIT),
        name="fox_prompt",
    )(qb, kb, vb, ck4)


def _merge(x, branches, gmix_ref, wgate_ref, bgate_ref, wout_ref):
    D = x.shape[1]
    u = _rms(x, gmix_ref[...]).astype(BF16)
    merged = None
    for j, (o, wo_ref) in enumerate(branches):
        sl = slice(j * D, (j + 1) * D)
        gate = jax.nn.sigmoid(_dot(u, wgate_ref[:, sl]) + bgate_ref[:, sl])
        term = gate * _dot(o, wo_ref[...])
        merged = term if merged is None else merged + term
    return x + _dot(merged.astype(BF16), wout_ref[...])


def _merge_ffn_prompt_kernel(x_ref, ofox_ref, olru_ref, omem_ref, gmix_ref, wgate_ref, bgate_ref,
                             wofox_ref, wolru_ref, womem_ref, wout_ref, gffn_ref, wfg_ref, wfu_ref,
                             wfc_ref, bfc_ref, wfd_ref, gfin_ref,
                             y_ref, gtail_ref, gext_scr):
    tm = x_ref.shape[1]
    dff = wfg_ref.shape[1]
    t = pl.program_id(1)

    @pl.when(t == 0)
    def _():
        gext_scr[0:SUBLANES, :] = jnp.zeros((SUBLANES, dff), F32)

    h = _merge(x_ref[0], [(ofox_ref[0], wofox_ref), (olru_ref[0], wolru_ref),
                          (omem_ref[0], womem_ref)], gmix_ref, wgate_ref, bgate_ref, wout_ref)
    v2 = _rms(h, gffn_ref[...]).astype(BF16)
    y = h
    for c in range(dff // FF_CHUNK):
        sl = slice(c * FF_CHUNK, (c + 1) * FF_CHUNK)
        g = _dot(v2, wfg_ref[:, sl])
        gext_scr[SUBLANES:SUBLANES + tm, sl] = g
        gc = bfc_ref[:, sl] + wfc_ref[CONV_FFN - 1:CONV_FFN, sl] * g
        for j in range(CONV_FFN - 1):
            off = SUBLANES - (CONV_FFN - 1) + j
            gc = gc + wfc_ref[j:j + 1, sl] * gext_scr[off:off + tm, sl]
        act = (_gelu(gc) * _dot(v2, wfu_ref[:, sl])).astype(BF16)
        y = y + _dot(act, wfd_ref[sl, :])
    tail = gext_scr[tm:tm + SUBLANES, :]
    gtail_ref[0] = tail
    gext_scr[0:SUBLANES, :] = tail
    y_ref[0] = _rms(y, gfin_ref[...])


def _merge_ffn_consts(w):
    return [w["g_mix"], w["w_gate"], w["b_gate"], w["w_o_fox"], w["w_o_lru"], w["w_o_mem"],
            w["w_out"], w["g_ffn"], w["w_ffn_gate"], w["w_ffn_up"], w["w_ffn_conv"],
            w["b_ffn_conv"], w["w_ffn_down"], w["g_final"]]


def _merge_ffn_prompt(x, ofox, olru, omem, w):
    B, S, D = x.shape
    dff = w["w_ffn_gate"].shape[1]
    tm = TM_FFN
    row = lambda width: pl.BlockSpec((1, tm, width), lambda b, t: (b, t, 0))
    consts = _merge_ffn_consts(w)
    return pl.pallas_call(
        _merge_ffn_prompt_kernel,
        grid=(B, S // tm),
        in_specs=[row(D), row(FOX_W), row(LRU_W), row(MEM_W)]
                 + [_const_spec(c.shape) for c in consts],
        out_specs=[row(D), pl.BlockSpec((1, SUBLANES, dff), lambda b, t: (b, 0, 0))],
        out_shape=[jax.ShapeDtypeStruct((B, S, D), F32),
                   jax.ShapeDtypeStruct((B, SUBLANES, dff), F32)],
        scratch_shapes=[pltpu.VMEM((tm + SUBLANES, dff), F32)],
        compiler_params=pltpu.CompilerParams(dimension_semantics=("arbitrary", "arbitrary"),
                                             vmem_limit_bytes=VMEM_LIMIT),
        name="merge_ffn_prompt",
    )(x, ofox, olru, omem, *consts)


def _proj_sample_kernel(x_ref, gmix_ref, wqkv_ref, wf_ref, bf_ref, wrnn_ref, wqm_ref,
                        wconv_ref, bconv_ref, wax_ref, bax_ref, lam_ref, cstate_ref, h0_ref,
                        q_ref, k_ref, v_ref, logf_ref, qm_ref, olru_ref, xtail_ref, hlast_ref,
                        x_scr, h_scr):
    nb = h0_ref.shape[0]
    nt = x_ref.shape[0] // nb
    u = _rms(x_ref[...], gmix_ref[...]).astype(BF16)
    zqkv = _dot(u, wqkv_ref[...])
    q_ref[...] = zqkv[:, :FOX_W]
    k_ref[...] = zqkv[:, FOX_W:2 * FOX_W]
    v_ref[...] = zqkv[:, 2 * FOX_W:]
    logf = _log_sigmoid(_dot(u, wf_ref[...]) + bf_ref[...])
    logf_ref[...] = logf[:, :N_HEADS]
    qm_ref[...] = _dot(u, wqm_ref[...]) * (MEM_DIM ** -0.5)

    zr = _dot(u, wrnn_ref[...])
    _park(x_scr, zr[:, :LRU_W])
    xs = [cstate_ref[j] for j in range(CONV_LRU - 1)]
    xs += [_token_rows(x_scr, t, nb, nt) for t in range(nt)]
    h = h0_ref[...]
    for t in range(nt):
        xc = bconv_ref[...]
        for j in range(CONV_LRU):
            xc = xc + wconv_ref[j:j + 1, :] * xs[t + j]
        a, gated = _lru_coeffs(xc, wax_ref, bax_ref, lam_ref)
        h = a * h + gated
        _set_token_rows(h_scr, t, nb, nt, h)
    hlast_ref[...] = h
    for j in range(CONV_LRU - 1):
        xtail_ref[j] = xs[nt + j]
    olru_ref[...] = (_unpark(h_scr) * _gelu(zr[:, LRU_W:])).astype(BF16)


def _proj_sample(x2, cstate_tm, h0, w):
    R, D = x2.shape
    nb = h0.shape[0]
    nt = R // nb
    rb = SAMPLE_ROWS
    sb = rb // nt
    consts = [w["g_mix"], w["w_qkv"], w["w_f"], w["b_f"], w["w_rnn"], w["w_qm"],
              w["w_lru_conv"], w["b_lru_conv"], w["w_ax"], w["b_ax"], w["lam"]]
    row = lambda width: pl.BlockSpec((rb, width), lambda i: (i, 0))
    state = pl.BlockSpec((CONV_LRU - 1, sb, LRU_W), lambda i: (0, i, 0))
    seq = pl.BlockSpec((sb, LRU_W), lambda i: (i, 0))
    out_shape = [
        jax.ShapeDtypeStruct((R, FOX_W), F32),
        jax.ShapeDtypeStruct((R, FOX_W), F32),
        jax.ShapeDtypeStruct((R, FOX_W), F32),
        jax.ShapeDtypeStruct((R, N_HEADS), F32),
        jax.ShapeDtypeStruct((R, MEM_W), F32),
        jax.ShapeDtypeStruct((R, LRU_W), BF16),
        jax.ShapeDtypeStruct((CONV_LRU - 1, nb, LRU_W), F32),
        jax.ShapeDtypeStruct((nb, LRU_W), F32),
    ]
    return pl.pallas_call(
        _proj_sample_kernel,
        grid=(R // rb,),
        in_specs=[row(D)] + [_const_spec(c.shape) for c in consts] + [state, seq],
        out_specs=[row(FOX_W), row(FOX_W), row(FOX_W), row(N_HEADS), row(MEM_W), row(LRU_W),
                   state, seq],
        out_shape=out_shape,
        scratch_shapes=[pltpu.VMEM((LRU_W // LANES, rb, LANES), F32)] * 2,
        compiler_params=pltpu.CompilerParams(dimension_semantics=("arbitrary",),
                                             vmem_limit_bytes=VMEM_LIMIT),
        name="proj_sample",
    )(x2, *consts, cstate_tm, h0)


def _mem_sample_kernel(q_ref, mk_ref, mv_ref, o_ref):
    nb = mk_ref.shape[0]
    nt = q_ref.shape[0] // nb
    nrow = MEM_HEADS * nt
    row_head = lax.broadcasted_iota(jnp.int32, (nrow, mk_ref.shape[1]), 0) // nt
    col_head = lax.broadcasted_iota(jnp.int32, (nrow, mk_ref.shape[1]), 1) % MEM_HEADS
    own = row_head == col_head
    for b in range(nb):
        rows = slice(b * nt, (b + 1) * nt)
        q = jnp.concatenate([q_ref[rows, h * MEM_DIM:(h + 1) * MEM_DIM] for h in range(MEM_HEADS)],
                            axis=0).astype(BF16)
        s = jnp.where(own, _dot_nt(q, mk_ref[b].astype(BF16)), NEG_BIG)
        p = jnp.exp(s - jnp.max(s, axis=-1, keepdims=True))
        o = _dot(p.astype(BF16), mv_ref[b].astype(BF16)) / jnp.sum(p, axis=-1, keepdims=True)
        for h in range(MEM_HEADS):
            o_ref[rows, h * MEM_DIM:(h + 1) * MEM_DIM] = o[h * nt:(h + 1) * nt, :]


def _mem_sample(qm, mk, mv):
    R, W = qm.shape
    DB, MH, Dm = mk.shape
    nt = R // DB
    nb = MEM_BATCH
    kv_spec = pl.BlockSpec((nb, MH, Dm), lambda i: (i, 0, 0))
    return pl.pallas_call(
        _mem_sample_kernel,
        grid=(DB // nb,),
        in_specs=[pl.BlockSpec((nb * nt, W), lambda i: (i, 0)), kv_spec, kv_spec],
        out_specs=pl.BlockSpec((nb * nt, W), lambda i: (i, 0)),
        out_shape=jax.ShapeDtypeStruct((R, W), F32),
        compiler_params=pltpu.CompilerParams(dimension_semantics=("arbitrary",),
                                             vmem_limit_bytes=VMEM_LIMIT),
        name="mem_sample",
    )(qm, mk, mv)


def _fox_sample_kernel(pt_ref, q_ref, kn_ref, vn_ref, lfn_ref, k_hbm, v_hbm, f_hbm, o_ref,
                       k_buf, v_buf, f_buf, k_sem, v_sem, f_sem,
                       kb_scr, vb_scr, qbd_scr, m_scr, l_scr, acc_scr, c_scr, pack_scr):
    npg = PAGES_PER_STEP
    nslot = k_buf.shape[0]
    nt = q_ref.shape[0]
    nrow = N_HEADS * nt
    c = pl.program_id(1)
    nch = pl.num_programs(1)
    step = pl.program_id(0) * nch + c
    last = pl.num_programs(0) * nch - 1
    lane = lax.broadcasted_iota(jnp.int32, (nt, FOX_W), 1)
    streams = ((k_hbm, k_buf, k_sem), (v_hbm, v_buf, v_sem), (f_hbm, f_buf, f_sem))

    def page_copies(slot, pages):
        return [pltpu.make_async_copy(hbm.at[pages[p]], buf.at[slot, p], sem.at[slot])
                for hbm, buf, sem in streams for p in range(npg)]

    def fetch(s):
        src = jnp.minimum(s, last)
        row = src // nch
        col = (src % nch) * npg
        for cp in page_copies(s % nslot, [pt_ref[row, col + p] for p in range(npg)]):
            cp.start(priority=PAGE_DMA_PRIORITY)

    def wait(slot):
        for cp in page_copies(slot, [0] * npg):
            cp.wait()

    @pl.when(step == 0)
    def _():
        for s in range(nslot - 1):
            fetch(jnp.int32(s))

    slot = step % nslot
    wait(slot)
    k_pages = [k_buf.at[slot, p] for p in range(npg)]
    v_pages = [v_buf.at[slot, p] for p in range(npg)]
    f_pages = [f_buf.at[slot, p] for p in range(npg)]

    @pl.when(c == 0)
    def _():
        q = q_ref[...]
        for h in range(N_HEADS):
            mine = (lane >= h * HEAD_DIM) & (lane < (h + 1) * HEAD_DIM)
            qbd_scr[h * nt:(h + 1) * nt, :] = jnp.where(mine, q, 0.0).astype(BF16)
        m_scr[...] = jnp.full_like(m_scr, NEG_BIG)
        l_scr[...] = jnp.zeros_like(l_scr)
        acc_scr[...] = jnp.zeros_like(acc_scr)
        c_scr[...] = jnp.zeros_like(c_scr)

    def update(s, pv):
        m = m_scr[...]
        m_new = jnp.maximum(m, jnp.max(s, axis=-1, keepdims=True))
        alpha = jnp.exp(m - m_new)
        p = jnp.exp(s - m_new)
        l_scr[...] = alpha * l_scr[...] + jnp.sum(p, axis=-1, keepdims=True)
        acc_scr[...] = alpha * acc_scr[...] + pv(p.astype(BF16))
        m_scr[...] = m_new

    def head_rows(cb):
        return jnp.concatenate(
            [jnp.broadcast_to(cb[h:h + 1, :], (nt, cb.shape[1])) for h in range(N_HEADS)], axis=0)

    z = _lane_cumsum(jnp.concatenate([f_pages[p][...] for p in range(npg)], axis=0), PAGE)
    tiles = [z[p * N_HEADS:(p + 1) * N_HEADS, :] for p in range(npg)]
    totals = [jnp.broadcast_to(t[:, PAGE - 1:PAGE], t.shape) for t in tiles]
    off = c_scr[...]
    for p in range(npg):
        tiles[p] = tiles[p] + off
        off = off + totals[p]
    c_scr[...] = off

    for p in range(npg):
        kb_scr[:, p * PAGE:(p + 1) * PAGE] = k_pages[p][...].astype(BF16)
        vb_scr[:, p * PAGE:(p + 1) * PAGE] = v_pages[p][...].astype(BF16)
    s = _dot(qbd_scr[...], kb_scr[...]) - head_rows(jnp.concatenate(tiles, axis=1))
    update(s, lambda p: _dot_nt(p, vb_scr[...]))
    fetch(step + (nslot - 1))

    @pl.when(c == pl.num_programs(1) - 1)
    def _():
        zeros = jnp.zeros((PAGE - nt, FOX_W), F32)
        kn = jnp.concatenate([kn_ref[...], zeros], axis=0).astype(BF16)
        vn = jnp.concatenate([vn_ref[...], zeros], axis=0).astype(BF16)
        pack_scr[...] = jnp.zeros_like(pack_scr)
        pack_scr[0:nt, 0:N_HEADS] = lfn_ref[0]
        cn = _lane_cumsum(pack_scr[...].T[:N_HEADS, :], PAGE) + c_scr[...]
        s = _dot_nt(qbd_scr[...], kn) - head_rows(cn)
        tok = lax.broadcasted_iota(jnp.int32, (nrow, PAGE), 0) % nt
        key = lax.broadcasted_iota(jnp.int32, (nrow, PAGE), 1)
        update(jnp.where(key <= tok, s, NEG_BIG), lambda p: _dot(p, vn))
        acc = acc_scr[...] / l_scr[...]
        out = acc[0:nt, :]
        for h in range(1, N_HEADS):
            out = jnp.where(lane >= h * HEAD_DIM, acc[h * nt:(h + 1) * nt, :], out)
        o_ref[...] = out

    @pl.when(step == last)
    def _():
        for ahead in range(1, nslot):
            wait((step + ahead) % nslot)


N_FOX_SAMPLE_IN = 7
N_FOX_SAMPLE_SCRATCH = 14
N_PROJ_PROMPT_IN = 14
N_PROJ_PROMPT_OUT = 11


def _fox_sample_proj_prompt_kernel(steps_per_tile, tiles_per_seq, pt_ref, *refs):
    bounds = [N_FOX_SAMPLE_IN, N_PROJ_PROMPT_IN, 1, N_PROJ_PROMPT_OUT, N_FOX_SAMPLE_SCRATCH]
    parts, at = [], 0
    for n in bounds:
        parts.append(refs[at:at + n])
        at += n
    fox_in, proj_in, fox_out, proj_out, fox_scratch = parts
    proj_scratch = refs[at:]
    _fox_sample_kernel(pt_ref, *fox_in, *fox_out, *fox_scratch)
    step = pl.program_id(0) * pl.num_programs(1) + pl.program_id(1)

    @pl.when(step % steps_per_tile == 0)
    def _():
        _proj_prompt_tile((step // steps_per_tile) % tiles_per_seq,
                          *proj_in, *proj_out, *proj_scratch)


def _fox_sample_proj_prompt(page_table, q, k_new, v_new, logf_new, cache_kt, cache_vt, cache_ft,
                            x, mk, mv, w):
    DB, npages = page_table.shape
    R, W = q.shape
    nt = R // DB
    npg = PAGES_PER_STEP
    nk = npg * PAGE
    nch = npages // npg
    nslot = FETCH_SLOTS
    tiles_per_seq = x.shape[1] // TM_PROJ
    steps_per_tile = (DB * nch) // (x.shape[0] * tiles_per_seq)
    assert steps_per_tile * x.shape[0] * tiles_per_seq == DB * nch

    last_tile = x.shape[0] * tiles_per_seq - 1

    def tile_of(b, c, pt, ahead=0):
        tile = jnp.minimum((b * nch + c + ahead) // steps_per_tile, last_tile)
        return tile // tiles_per_seq, tile % tiles_per_seq

    proj_args, proj_in_specs, proj_out_specs, proj_out_shape, proj_scratch = (
        _proj_prompt_call_parts(x, mk, mv, w, functools.partial(tile_of, ahead=nslot - 1), tile_of))
    assert (len(proj_args), len(proj_out_specs)) == (N_PROJ_PROMPT_IN, N_PROJ_PROMPT_OUT)
    tok = pl.BlockSpec((nt, W), lambda b, c, pt: (b, 0))
    hbm = pl.BlockSpec(memory_space=pl.ANY)
    fox_in_specs = [tok, tok, tok, pl.BlockSpec((1, nt, N_HEADS), lambda b, c, pt: (b, 0, 0)),
                    hbm, hbm, hbm]
    fox_scratch = [pltpu.VMEM((nslot, npg, W, PAGE), F32),
                   pltpu.VMEM((nslot, npg, W, PAGE), F32),
                   pltpu.VMEM((nslot, npg, N_HEADS, PAGE), F32),
                   pltpu.SemaphoreType.DMA((nslot,)), pltpu.SemaphoreType.DMA((nslot,)),
                   pltpu.SemaphoreType.DMA((nslot,)),
                   pltpu.VMEM((W, nk), BF16), pltpu.VMEM((W, nk), BF16),
                   pltpu.VMEM((N_HEADS * nt, W), BF16),
                   pltpu.VMEM((N_HEADS * nt, 1), F32), pltpu.VMEM((N_HEADS * nt, 1), F32),
                   pltpu.VMEM((N_HEADS * nt, W), F32),
                   pltpu.VMEM((N_HEADS, PAGE), F32), pltpu.VMEM((PAGE, LANES), F32)]
    assert (len(fox_in_specs), len(fox_scratch)) == (N_FOX_SAMPLE_IN, N_FOX_SAMPLE_SCRATCH)
    grid_spec = pltpu.PrefetchScalarGridSpec(
        num_scalar_prefetch=1,
        grid=(DB, nch),
        in_specs=fox_in_specs + proj_in_specs,
        out_specs=[tok] + proj_out_specs,
        scratch_shapes=fox_scratch + proj_scratch,
    )
    return pl.pallas_call(
        functools.partial(_fox_sample_proj_prompt_kernel, steps_per_tile, tiles_per_seq),
        grid_spec=grid_spec,
        out_shape=[jax.ShapeDtypeStruct((R, W), F32)] + proj_out_shape,
        compiler_params=pltpu.CompilerParams(dimension_semantics=("arbitrary", "arbitrary"),
                                             vmem_limit_bytes=VMEM_LIMIT),
        name="fox_sample_proj_prompt",
    )(page_table, q, k_new, v_new, logf_new, cache_kt, cache_vt, cache_ft, *proj_args)


def _merge_ffn_sample_kernel(x_ref, ofox_ref, olru_ref, omem_ref, fstate_ref, gmix_ref, wgate_ref,
                             bgate_ref, wofox_ref, wolru_ref, womem_ref, wout_ref, gffn_ref,
                             wfg_ref, wfu_ref, wfc_ref, bfc_ref, wfd_ref, gfin_ref,
                             y_ref, gtail_ref, g_scr, gc_scr):
    nb = fstate_ref.shape[1]
    nt = x_ref.shape[0] // nb
    dff = wfg_ref.shape[1]
    h = _merge(x_ref[...], [(ofox_ref[...].astype(BF16), wofox_ref), (olru_ref[...], wolru_ref),
                            (omem_ref[...].astype(BF16), womem_ref)],
               gmix_ref, wgate_ref, bgate_ref, wout_ref)
    v2 = _rms(h, gffn_ref[...]).astype(BF16)
    y = h
    for c in range(dff // FF_CHUNK):
        sl = slice(c * FF_CHUNK, (c + 1) * FF_CHUNK)
        _park(g_scr, _dot(v2, wfg_ref[:, sl]))
        gs = [fstate_ref[j, :, sl] for j in range(CONV_FFN - 1)]
        gs += [_token_rows(g_scr, t, nb, nt) for t in range(nt)]
        for t in range(nt):
            gc = bfc_ref[:, sl]
            for j in range(CONV_FFN):
                gc = gc + wfc_ref[j:j + 1, sl] * gs[t + j]
            _set_token_rows(gc_scr, t, nb, nt, gc)
        for j in range(CONV_FFN - 1):
            gtail_ref[j, :, sl] = gs[nt + j]
        act = (_gelu(_unpark(gc_scr)) * _dot(v2, wfu_ref[:, sl])).astype(BF16)
        y = y + _dot(act, wfd_ref[sl, :])
    y_ref[...] = _rms(y, gfin_ref[...])


def _merge_ffn_sample(x2, ofox, olru, omem, fstate_tm, w):
    R, D = x2.shape
    nb = fstate_tm.shape[1]
    nt = R // nb
    dff = w["w_ffn_gate"].shape[1]
    rb = SAMPLE_ROWS
    sb = rb // nt
    consts = _merge_ffn_consts(w)
    row = lambda width: pl.BlockSpec((rb, width), lambda i: (i, 0))
    state = pl.BlockSpec((CONV_FFN - 1, sb, dff), lambda i: (0, i, 0))
    return pl.pallas_call(
        _merge_ffn_sample_kernel,
        grid=(R // rb,),
        in_specs=[row(D), row(FOX_W), row(LRU_W), row(MEM_W), state]
                 + [_const_spec(c.shape) for c in consts],
        out_specs=[row(D), state],
        out_shape=[jax.ShapeDtypeStruct((R, D), F32),
                   jax.ShapeDtypeStruct((CONV_FFN - 1, nb, dff), F32)],
        scratch_shapes=[pltpu.VMEM((FF_CHUNK // LANES, rb, LANES), F32)] * 2,
        compiler_params=pltpu.CompilerParams(dimension_semantics=("arbitrary",),
                                             vmem_limit_bytes=VMEM_LIMIT),
        name="merge_ffn_sample",
    )(x2, ofox, olru, omem, fstate_tm, *consts)


def _block_diag(wb):
    nb, bs, _ = wb.shape
    eye = jnp.eye(nb, dtype=wb.dtype)
    return (eye[:, None, :, None] * wb[:, :, None, :]).reshape(nb * bs, nb * bs)


def _prep_weights(l, g_mix, w_in, b_f, w_o_fox, w_lru_conv, b_lru_conv, w_lru_a, b_lru_a, w_lru_x,
                  b_lru_x, lru_lambda, w_o_lru, w_o_mem, w_gate, b_gate, w_out, g_ffn, w_ffn_gate,
                  w_ffn_up, w_ffn_conv, b_ffn_conv, w_ffn_down, g_final):
    wi = w_in[l]
    o_f = 3 * FOX_W
    o_r = o_f + N_HEADS
    o_m = o_r + 2 * LRU_W
    row = lambda v: v.reshape(1, -1)
    w_qkv = jnp.concatenate([wi[:, :FOX_W] * (HEAD_DIM ** -0.5), wi[:, FOX_W:o_f]], axis=1)
    return {
        "g_mix": row(g_mix[l]),
        "w_qkv": w_qkv.astype(BF16),
        "w_f": jnp.pad(wi[:, o_f:o_r], ((0, 0), (0, LANES - N_HEADS))).astype(BF16),
        "b_f": jnp.pad(row(b_f[l]), ((0, 0), (0, LANES - N_HEADS))),
        "w_rnn": wi[:, o_r:o_m].astype(BF16),
        "w_qm": wi[:, o_m:].astype(BF16),
        "w_lru_conv": w_lru_conv[l],
        "b_lru_conv": row(b_lru_conv[l]),
        "w_ax": jnp.concatenate([_block_diag(w_lru_a[l]), _block_diag(w_lru_x[l])],
                                axis=1).astype(BF16),
        "b_ax": jnp.concatenate([row(b_lru_a[l]), row(b_lru_x[l])], axis=1),
        "lam": row(lru_lambda[l]),
        "w_gate": w_gate[l].astype(BF16),
        "b_gate": row(b_gate[l]),
        "w_o_fox": w_o_fox[l].astype(BF16),
        "w_o_lru": w_o_lru[l].astype(BF16),
        "w_o_mem": w_o_mem[l].astype(BF16),
        "w_out": w_out[l].astype(BF16),
        "g_ffn": row(g_ffn[l]),
        "w_ffn_gate": w_ffn_gate[l].astype(BF16),
        "w_ffn_up": w_ffn_up[l].astype(BF16),
        "w_ffn_conv": w_ffn_conv[l],
        "b_ffn_conv": row(b_ffn_conv[l]),
        "w_ffn_down": w_ffn_down[l].astype(BF16),
        "g_final": row(g_final),
    }


def kernel(x_prompt, x_sample, mem_prompt, cache_k, cache_v, cache_logf, cache_mem_k, cache_mem_v,
           state_lru_h, state_lru_conv, state_ffn_conv, page_table,
           g_mix, w_in, b_f, w_o_fox, w_lru_conv, b_lru_conv, w_lru_a, b_lru_a, w_lru_x, b_lru_x,
           lru_lambda, w_o_lru, g_mem, w_mem_kv, w_o_mem, w_gate, b_gate, w_out,
           g_ffn, w_ffn_gate, w_ffn_up, w_ffn_conv, b_ffn_conv, w_ffn_down, g_final):
    depth = w_in.shape[0]
    assert depth == 1, "the final norm is fused into the single layer"
    l = 0
    B, S, D = x_prompt.shape
    DB, T, _ = x_sample.shape
    n_pool = cache_k.shape[1]
    w = _prep_weights(l, g_mix, w_in, b_f, w_o_fox, w_lru_conv, b_lru_conv, w_lru_a, b_lru_a,
                      w_lru_x, b_lru_x, lru_lambda, w_o_lru, w_o_mem, w_gate, b_gate, w_out,
                      g_ffn, w_ffn_gate, w_ffn_up, w_ffn_conv, b_ffn_conv, w_ffn_down, g_final)

    mk_p, mv_p = _mem_kv(mem_prompt, g_mem[l].reshape(1, -1), w_mem_kv[l].astype(BF16))

    x2 = x_sample.reshape(DB * T, D)
    cstate_tm = jnp.swapaxes(state_lru_conv[l], 0, 1)
    fstate_tm = jnp.swapaxes(state_ffn_conv[l], 0, 1)
    (q_s, k_s, v_s, logf_s, qm_s, olru_s, xtail_s, hlast_s) = _proj_sample(
        x2, cstate_tm, state_lru_h[l], w)
    omem_s = _mem_sample(qm_s, cache_mem_k[l].reshape(DB, -1, MEM_DIM),
                         cache_mem_v[l].reshape(DB, -1, MEM_DIM))
    cache_kt = jnp.transpose(cache_k[l], (0, 2, 3, 1)).reshape(n_pool, FOX_W, PAGE)
    cache_vt = jnp.transpose(cache_v[l], (0, 2, 3, 1)).reshape(n_pool, FOX_W, PAGE)
    cache_ft = jnp.transpose(cache_logf[l], (0, 2, 1))
    (ofox_s, k_p, v_p, logf_p, ck_p, qb, kb, vb, olru_p, omem_p, xtail_p, htail_p) = (
        _fox_sample_proj_prompt(page_table, q_s, k_s, v_s, logf_s.reshape(DB, T, N_HEADS),
                                cache_kt, cache_vt, cache_ft, x_prompt, mk_p, mv_p, w))
    y_s, gtail_s = _merge_ffn_sample(x2, ofox_s, olru_s, omem_s, fstate_tm, w)

    ofox_p = _fox_prompt(qb, kb, vb, ck_p)
    y_p, gtail_p = _merge_ffn_prompt(x_prompt, ofox_p, olru_p, omem_p, w)

    heads = lambda a, n: a.reshape(1, n, -1, N_HEADS, HEAD_DIM)
    mem_heads = lambda a: a.reshape(1, B, -1, MEM_HEADS, MEM_DIM)
    return (
        y_p,
        y_s.reshape(DB, T, D),
        heads(k_p, B), heads(v_p, B), logf_p[None],
        mem_heads(mk_p), mem_heads(mv_p),
        htail_p[None, :, -1, :],
        xtail_p[None, :, SUBLANES - (CONV_LRU - 1):, :],
        gtail_p[None, :, SUBLANES - (CONV_FFN - 1):, :],
        heads(k_s, DB), heads(v_s, DB), logf_s.reshape(1, DB, T, N_HEADS),
        hlast_s[None],
        jnp.swapaxes(xtail_s, 0, 1)[None],
        jnp.swapaxes(gtail_s, 0, 1)[None],
    )
```

```python
import functools

import jax
import jax.numpy as jnp
from jax import lax
from jax.experimental import pallas as pl
from jax.experimental.pallas import tpu as pltpu

F32 = jnp.float32
BF16 = jnp.bfloat16

EPS = 1e-6
LRU_C = 8.0
NEG_BIG = -1e30
LANES = 128
SUBLANES = 8
VMEM_LIMIT = 56 * 1024 * 1024

N_HEADS = 8
HEAD_DIM = 64
FOX_W = N_HEADS * HEAD_DIM
LRU_W = 512
MEM_HEADS = 4
MEM_DIM = 128
MEM_W = MEM_HEADS * MEM_DIM
CONV_LRU = 4
CONV_FFN = 3
PAGE = 128

TM_PROJ = 128
TM_FFN = 256
TQ = 512
FF_CHUNK = 1024
PAGES_PER_STEP = 16
FETCH_SLOTS = 3
PAGE_DMA_PRIORITY = 1
MEM_BATCH = 8
SAMPLE_ROWS = 256


def _rms(x, g):
    return x * lax.rsqrt(jnp.mean(x * x, axis=-1, keepdims=True) + EPS) * g


def _log_sigmoid(x):
    return jnp.minimum(x, 0.0) - jnp.log1p(jnp.exp(-jnp.abs(x)))


def _gelu(x):
    return 0.5 * x * (1.0 + jnp.tanh(0.7978845608028654 * (x + 0.044715 * (x * x * x))))


def _dot(a, b):
    return jnp.dot(a, b, preferred_element_type=F32)


def _dot_nt(a, b):
    return lax.dot_general(a, b, (((1,), (1,)), ((), ())), preferred_element_type=F32)


def _lru_coeffs(xc, wax_ref, bax_ref, lam_ref):
    gates = _dot(xc.astype(BF16), wax_ref[...]) + bax_ref[...]
    r = jax.nn.sigmoid(gates[:, :LRU_W])
    i = jax.nn.sigmoid(gates[:, LRU_W:])
    log_a = LRU_C * r * _log_sigmoid(lam_ref[...])
    a = jnp.exp(log_a)
    th = jnp.tanh(log_a)
    gated = jnp.sqrt(-2.0 * th / (1.0 - th)) * i * xc
    return a, gated


def _mem_attend_head(q, mk, mv):
    s = _dot_nt(q, mk)
    m = jnp.max(s, axis=-1, keepdims=True)
    p = jnp.exp(s - m)
    l = jnp.sum(p, axis=-1, keepdims=True)
    return _dot(p.astype(BF16), mv) / l


def _lane_cumsum(x, width):
    lane = lax.broadcasted_iota(jnp.int32, x.shape, x.ndim - 1)
    d = 1
    while d < width:
        x = x + jnp.where(lane >= d, pltpu.roll(x, d, axis=x.ndim - 1), 0.0)
        d *= 2
    return x


def _park(scr, x):
    for c in range(scr.shape[0]):
        scr[c] = x[:, c * LANES:(c + 1) * LANES]


def _unpark(scr):
    return jnp.concatenate([scr[c] for c in range(scr.shape[0])], axis=1)


def _token_rows(scr, t, nseq, ntok):
    return jnp.concatenate([scr[c, pl.ds(t, nseq, stride=ntok), :] for c in range(scr.shape[0])],
                           axis=1)


def _set_token_rows(scr, t, nseq, ntok, x):
    for c in range(scr.shape[0]):
        scr[c, pl.ds(t, nseq, stride=ntok), :] = x[:, c * LANES:(c + 1) * LANES]


def _const_spec(shape):
    nd = len(shape)
    return pl.BlockSpec(shape, lambda *_: (0,) * nd, pipeline_mode=pl.Buffered(1))


def _mem_kv_kernel(mem_ref, g_ref, w_ref, mk_ref, mv_ref):
    u = _rms(mem_ref[0], g_ref[...]).astype(BF16)
    z = _dot(u, w_ref[...])
    mk_ref[0] = z[:, :MEM_W]
    mv_ref[0] = z[:, MEM_W:]


def _mem_kv(mem, g_mem, w_mem_kv):
    B, M, D = mem.shape
    return pl.pallas_call(
        _mem_kv_kernel,
        grid=(B,),
        in_specs=[pl.BlockSpec((1, M, D), lambda b: (b, 0, 0)),
                  _const_spec((1, D)), _const_spec((D, 2 * MEM_W))],
        out_specs=[pl.BlockSpec((1, M, MEM_W), lambda b: (b, 0, 0))] * 2,
        out_shape=[jax.ShapeDtypeStruct((B, M, MEM_W), F32)] * 2,
        compiler_params=pltpu.CompilerParams(dimension_semantics=("arbitrary",),
                                             vmem_limit_bytes=VMEM_LIMIT),
        name="mem_kv",
    )(mem, g_mem, w_mem_kv)


def _proj_prompt_tile(t, x_ref, gmix_ref, wqkv_ref, wf_ref, bf_ref, wrnn_ref, wqm_ref,
                      wconv_ref, bconv_ref, wax_ref, bax_ref, lam_ref, mk_ref, mv_ref,
                      k_ref, v_ref, logf_ref, ck_ref, qb_ref, kb_ref, vb_ref,
                      olru_ref, omem_ref, xtail_ref, htail_ref,
                      xext_scr, a_scr, g_scr, h_scr, c_scr):
    tm = x_ref.shape[1]
    pad = a_scr.shape[0] - tm

    @pl.when(t == 0)
    def _():
        xext_scr[0:SUBLANES, :] = jnp.zeros((SUBLANES, LRU_W), F32)
        h_scr[...] = jnp.zeros_like(h_scr)
        c_scr[...] = jnp.zeros_like(c_scr)
        a_scr[0:pad, :] = jnp.ones((pad, LRU_W), F32)
        g_scr[0:pad, :] = jnp.zeros((pad, LRU_W), F32)

    u = _rms(x_ref[0], gmix_ref[...]).astype(BF16)

    zqkv = _dot(u, wqkv_ref[...])
    k_ref[0] = zqkv[:, FOX_W:2 * FOX_W]
    v_ref[0] = zqkv[:, 2 * FOX_W:]
    qb_ref[0] = zqkv[:, :FOX_W].astype(BF16)
    kb_ref[0] = zqkv[:, FOX_W:2 * FOX_W].astype(BF16)
    vb_ref[0] = zqkv[:, 2 * FOX_W:].astype(BF16)

    logf = _log_sigmoid(_dot(u, wf_ref[...]) + bf_ref[...])
    logf_ref[0] = logf[:, :N_HEADS]
    c = _lane_cumsum(logf.T[:N_HEADS, :], tm) + c_scr[:, 0:1]
    ck_ref[0] = c
    c_scr[...] = jnp.broadcast_to(c[:, tm - 1:tm], c_scr.shape)

    zr = _dot(u, wrnn_ref[...])
    xr = zr[:, :LRU_W]
    xext_scr[SUBLANES:SUBLANES + tm, :] = xr
    xtail_ref[0] = xr[tm - SUBLANES:, :]
    xc = bconv_ref[...] + wconv_ref[CONV_LRU - 1:CONV_LRU, :] * xr
    for j in range(CONV_LRU - 1):
        off = SUBLANES - (CONV_LRU - 1) + j
        xc = xc + wconv_ref[j:j + 1, :] * xext_scr[off:off + tm, :]
    xext_scr[0:SUBLANES, :] = xr[tm - SUBLANES:, :]

    a, gated = _lru_coeffs(xc, wax_ref, bax_ref, lam_ref)
    d = 1
    while d < tm:
        a_scr[pad:pad + tm, :] = a
        g_scr[pad:pad + tm, :] = gated
        gated = gated + a * g_scr[pad - d:pad - d + tm, :]
        a = a * a_scr[pad - d:pad - d + tm, :]
        d *= 2
    hs = gated + a * h_scr[0:1, :]
    h_scr[...] = jnp.broadcast_to(hs[tm - 1:tm, :], h_scr.shape)
    htail_ref[0] = hs[tm - SUBLANES:, :]
    olru_ref[0] = (hs * _gelu(zr[:, LRU_W:])).astype(BF16)

    zq = _dot(u, wqm_ref[...]) * (MEM_DIM ** -0.5)
    for h in range(MEM_HEADS):
        sl = slice(h * MEM_DIM, (h + 1) * MEM_DIM)
        o = _mem_attend_head(zq[:, sl].astype(BF16), mk_ref[0, :, sl].astype(BF16),
                             mv_ref[0, :, sl].astype(BF16))
        omem_ref[0, :, sl] = o.astype(BF16)


def _proj_prompt_call_parts(x, mk, mv, w, in_tile_of, out_tile_of):
    B, S, D = x.shape
    M = mk.shape[1]
    tm = TM_PROJ

    def spec(block, place, tile_of=out_tile_of):
        return pl.BlockSpec(block, lambda *g: place(*tile_of(*g)))

    row = lambda width: spec((1, tm, width), lambda b, t: (b, t, 0))
    per_b = lambda rows, width: spec((1, rows, width), lambda b, t: (b, 0, 0))
    consts = [w["g_mix"], w["w_qkv"], w["w_f"], w["b_f"], w["w_rnn"], w["w_qm"],
              w["w_lru_conv"], w["b_lru_conv"], w["w_ax"], w["b_ax"], w["lam"]]
    out_shape = [
        jax.ShapeDtypeStruct((B, S, FOX_W), F32),
        jax.ShapeDtypeStruct((B, S, FOX_W), F32),
        jax.ShapeDtypeStruct((B, S, N_HEADS), F32),
        jax.ShapeDtypeStruct((B, N_HEADS, S), F32),
        jax.ShapeDtypeStruct((B, S, FOX_W), BF16),
        jax.ShapeDtypeStruct((B, S, FOX_W), BF16),
        jax.ShapeDtypeStruct((B, S, FOX_W), BF16),
        jax.ShapeDtypeStruct((B, S, LRU_W), BF16),
        jax.ShapeDtypeStruct((B, S, MEM_W), BF16),
        jax.ShapeDtypeStruct((B, SUBLANES, LRU_W), F32),
        jax.ShapeDtypeStruct((B, SUBLANES, LRU_W), F32),
    ]
    out_specs = [row(FOX_W), row(FOX_W), row(N_HEADS),
                 spec((1, N_HEADS, tm), lambda b, t: (b, 0, t)),
                 row(FOX_W), row(FOX_W), row(FOX_W), row(LRU_W), row(MEM_W),
                 per_b(SUBLANES, LRU_W), per_b(SUBLANES, LRU_W)]
    pad = tm // 2
    in_specs = ([spec((1, tm, D), lambda b, t: (b, t, 0), in_tile_of)]
                + [_const_spec(c.shape) for c in consts]
                + [spec((1, M, MEM_W), lambda b, t: (b, 0, 0), in_tile_of)] * 2)
    scratch_shapes = [pltpu.VMEM((tm + SUBLANES, LRU_W), F32),
                      pltpu.VMEM((tm + pad, LRU_W), F32),
                      pltpu.VMEM((tm + pad, LRU_W), F32),
                      pltpu.VMEM((SUBLANES, LRU_W), F32),
                      pltpu.VMEM((N_HEADS, LANES), F32)]
    return [x, *consts, mk, mv], in_specs, out_specs, out_shape, scratch_shapes


def _fox_prompt_kernel(q_ref, k_ref, v_ref, ck_ref, o_ref, s_scr, p_scr, m_scr, acc_scr):
    tq = q_ref.shape[1]
    i = pl.program_id(2)
    q2 = q_ref[0]
    first = lax.broadcasted_iota(jnp.int32, q2.shape, 1) < HEAD_DIM
    zero = jnp.zeros_like(q2)
    qs = jnp.concatenate([jnp.where(first, q2, zero), jnp.where(first, zero, q2)], axis=0)
    ones = jnp.ones((tq, LANES), BF16)

    def logits(j):
        return _dot_nt(qs, k_ref[0, pl.ds(pl.multiple_of(j * tq, tq), tq), :])

    def weighted_values(p, j):
        v_aug = jnp.concatenate([v_ref[0, pl.ds(pl.multiple_of(j * tq, tq), tq), :], ones], axis=1)
        return _dot(p, v_aug)

    def softmax_block(j, slot, pv, masked):
        start = pl.multiple_of(j * tq, tq)
        if masked:
            causal = (lax.broadcasted_iota(jnp.int32, (tq, tq), 1)
                      <= lax.broadcasted_iota(jnp.int32, (tq, tq), 0))
        for hh in range(2):
            rows = slice(hh * tq, (hh + 1) * tq)
            s = s_scr[slot, rows, :] - ck_ref[0, 0, hh:hh + 1, pl.ds(start, tq)]
            if masked:
                s = jnp.where(causal, s, NEG_BIG)
            m_new = jnp.max(s, axis=-1, keepdims=True)
            if pv is None:
                acc_scr[rows, :] = jnp.zeros((tq, acc_scr.shape[1]), F32)
            else:
                m_old = m_scr[rows, :]
                m_new = jnp.maximum(m_old, m_new)
                acc_scr[rows, :] = jnp.exp(m_old - m_new) * (acc_scr[rows, :] + pv[rows, :])
            p_scr[slot, rows, :] = jnp.exp(s - m_new).astype(BF16)
            m_scr[rows, :] = m_new

    def step(t, slot):
        other = 1 - slot
        pv = weighted_values(p_scr[other], t - 1)
        s_scr[other] = logits(t + 1)
        softmax_block(t, slot, pv, False)

    def finish(slot, pv):
        softmax_block(i, slot, pv, True)
        acc = acc_scr[...] + weighted_values(p_scr[slot], i)
        out = acc[:, :LANES] / acc[:, LANES:]
        o_ref[0] = jnp.where(first, out[:tq, :], out[tq:, :]).astype(o_ref.dtype)

    s_scr[0] = logits(0)

    @pl.when(i == 0)
    def _():
        finish(0, None)

    @pl.when(i > 0)
    def _():
        s_scr[1] = logits(1)
        softmax_block(0, 0, None, False)

        def two_steps(tt, carry):
            step(2 * tt + 1, 1)
            step(2 * tt + 2, 0)
            return carry

        lax.fori_loop(0, (i - 1) // 2, two_steps, 0)

        @pl.when(i % 2 == 0)
        def _():
            step(i - 1, 1)
            finish(0, weighted_values(p_scr[1], i - 1))

        @pl.when(i % 2 == 1)
        def _():
            finish(1, weighted_values(p_scr[0], i - 1))


def _fox_prompt(qb, kb, vb, ck):
    B, S, W = qb.shape
    npair = W // LANES
    ck4 = ck.reshape(B, npair, 2, S)
    return pl.pallas_call(
        _fox_prompt_kernel,
        grid=(B, npair, S // TQ),
        in_specs=[pl.BlockSpec((1, TQ, LANES), lambda b, h, i: (b, i, h)),
                  pl.BlockSpec((1, S, LANES), lambda b, h, i: (b, 0, h)),
                  pl.BlockSpec((1, S, LANES), lambda b, h, i: (b, 0, h)),
                  pl.BlockSpec((1, 1, 2, S), lambda b, h, i: (b, h, 0, 0))],
        out_specs=pl.BlockSpec((1, TQ, LANES), lambda b, h, i: (b, i, h)),
        out_shape=jax.ShapeDtypeStruct((B, S, W), BF16),
        scratch_shapes=[pltpu.VMEM((2, 2 * TQ, TQ), F32), pltpu.VMEM((2, 2 * TQ, TQ), BF16),
                        pltpu.VMEM((2 * TQ, 1), F32), pltpu.VMEM((2 * TQ, 2 * LANES), F32)],
        compiler_params=pltpu.CompilerParams(
            dimension_semantics=("arbitrary", "arbitrary", "arbitrary"),
            vmem_limit_bytes=VMEM_LIMIT),
        name="fox_prompt",
    )(qb, kb, vb, ck4)


def _merge(x, branches, gmix_ref, wgate_ref, bgate_ref, wout_ref):
    D = x.shape[1]
    u = _rms(x, gmix_ref[...]).astype(BF16)
    merged = None
    for j, (o, wo_ref) in enumerate(branches):
        sl = slice(j * D, (j + 1) * D)
        gate = jax.nn.sigmoid(_dot(u, wgate_ref[:, sl]) + bgate_ref[:, sl])
        term = gate * _dot(o, wo_ref[...])
        merged = term if merged is None else merged + term
    return x + _dot(merged.astype(BF16), wout_ref[...])


def _merge_ffn_prompt_kernel(x_ref, ofox_ref, olru_ref, omem_ref, gmix_ref, wgate_ref, bgate_ref,
                             wofox_ref, wolru_ref, womem_ref, wout_ref, gffn_ref, wfg_ref, wfu_ref,
                             wfc_ref, bfc_ref, wfd_ref, gfin_ref,
                             y_ref, gtail_ref, gext_scr):
    tm = x_ref.shape[1]
    dff = wfg_ref.shape[1]
    t = pl.program_id(1)

    @pl.when(t == 0)
    def _():
        gext_scr[0:SUBLANES, :] = jnp.zeros((SUBLANES, dff), F32)

    h = _merge(x_ref[0], [(ofox_ref[0], wofox_ref), (olru_ref[0], wolru_ref),
                          (omem_ref[0], womem_ref)], gmix_ref, wgate_ref, bgate_ref, wout_ref)
    v2 = _rms(h, gffn_ref[...]).astype(BF16)
    y = h
    for c in range(dff // FF_CHUNK):
        sl = slice(c * FF_CHUNK, (c + 1) * FF_CHUNK)
        g = _dot(v2, wfg_ref[:, sl])
        gext_scr[SUBLANES:SUBLANES + tm, sl] = g
        gc = bfc_ref[:, sl] + wfc_ref[CONV_FFN - 1:CONV_FFN, sl] * g
        for j in range(CONV_FFN - 1):
            off = SUBLANES - (CONV_FFN - 1) + j
            gc = gc + wfc_ref[j:j + 1, sl] * gext_scr[off:off + tm, sl]
        act = (_gelu(gc) * _dot(v2, wfu_ref[:, sl])).astype(BF16)
        y = y + _dot(act, wfd_ref[sl, :])
    tail = gext_scr[tm:tm + SUBLANES, :]
    gtail_ref[0] = tail
    gext_scr[0:SUBLANES, :] = tail
    y_ref[0] = _rms(y, gfin_ref[...])


def _merge_ffn_consts(w):
    return [w["g_mix"], w["w_gate"], w["b_gate"], w["w_o_fox"], w["w_o_lru"], w["w_o_mem"],
            w["w_out"], w["g_ffn"], w["w_ffn_gate"], w["w_ffn_up"], w["w_ffn_conv"],
            w["b_ffn_conv"], w["w_ffn_down"], w["g_final"]]


def _merge_ffn_prompt(x, ofox, olru, omem, w):
    B, S, D = x.shape
    dff = w["w_ffn_gate"].shape[1]
    tm = TM_FFN
    row = lambda width: pl.BlockSpec((1, tm, width), lambda b, t: (b, t, 0))
    consts = _merge_ffn_consts(w)
    return pl.pallas_call(
        _merge_ffn_prompt_kernel,
        grid=(B, S // tm),
        in_specs=[row(D), row(FOX_W), row(LRU_W), row(MEM_W)]
                 + [_const_spec(c.shape) for c in consts],
        out_specs=[row(D), pl.BlockSpec((1, SUBLANES, dff), lambda b, t: (b, 0, 0))],
        out_shape=[jax.ShapeDtypeStruct((B, S, D), F32),
                   jax.ShapeDtypeStruct((B, SUBLANES, dff), F32)],
        scratch_shapes=[pltpu.VMEM((tm + SUBLANES, dff), F32)],
        compiler_params=pltpu.CompilerParams(dimension_semantics=("arbitrary", "arbitrary"),
                                             vmem_limit_bytes=VMEM_LIMIT),
        name="merge_ffn_prompt",
    )(x, ofox, olru, omem, *consts)


def _proj_sample_kernel(x_ref, gmix_ref, wqkv_ref, wf_ref, bf_ref, wrnn_ref, wqm_ref,
                        wconv_ref, bconv_ref, wax_ref, bax_ref, lam_ref, cstate_ref, h0_ref,
                        q_ref, k_ref, v_ref, logf_ref, qm_ref, olru_ref, xtail_ref, hlast_ref,
                        x_scr, h_scr):
    nb = h0_ref.shape[0]
    nt = x_ref.shape[0] // nb
    u = _rms(x_ref[...], gmix_ref[...]).astype(BF16)
    zqkv = _dot(u, wqkv_ref[...])
    q_ref[...] = zqkv[:, :FOX_W]
    k_ref[...] = zqkv[:, FOX_W:2 * FOX_W]
    v_ref[...] = zqkv[:, 2 * FOX_W:]
    logf = _log_sigmoid(_dot(u, wf_ref[...]) + bf_ref[...])
    logf_ref[...] = logf[:, :N_HEADS]
    qm_ref[...] = _dot(u, wqm_ref[...]) * (MEM_DIM ** -0.5)

    zr = _dot(u, wrnn_ref[...])
    _park(x_scr, zr[:, :LRU_W])
    xs = [cstate_ref[j] for j in range(CONV_LRU - 1)]
    xs += [_token_rows(x_scr, t, nb, nt) for t in range(nt)]
    h = h0_ref[...]
    for t in range(nt):
        xc = bconv_ref[...]
        for j in range(CONV_LRU):
            xc = xc + wconv_ref[j:j + 1, :] * xs[t + j]
        a, gated = _lru_coeffs(xc, wax_ref, bax_ref, lam_ref)
        h = a * h + gated
        _set_token_rows(h_scr, t, nb, nt, h)
    hlast_ref[...] = h
    for j in range(CONV_LRU - 1):
        xtail_ref[j] = xs[nt + j]
    olru_ref[...] = (_unpark(h_scr) * _gelu(zr[:, LRU_W:])).astype(BF16)


def _proj_sample(x2, cstate_tm, h0, w):
    R, D = x2.shape
    nb = h0.shape[0]
    nt = R // nb
    rb = SAMPLE_ROWS
    sb = rb // nt
    consts = [w["g_mix"], w["w_qkv"], w["w_f"], w["b_f"], w["w_rnn"], w["w_qm"],
              w["w_lru_conv"], w["b_lru_conv"], w["w_ax"], w["b_ax"], w["lam"]]
    row = lambda width: pl.BlockSpec((rb, width), lambda i: (i, 0))
    state = pl.BlockSpec((CONV_LRU - 1, sb, LRU_W), lambda i: (0, i, 0))
    seq = pl.BlockSpec((sb, LRU_W), lambda i: (i, 0))
    out_shape = [
        jax.ShapeDtypeStruct((R, FOX_W), F32),
        jax.ShapeDtypeStruct((R, FOX_W), F32),
        jax.ShapeDtypeStruct((R, FOX_W), F32),
        jax.ShapeDtypeStruct((R, N_HEADS), F32),
        jax.ShapeDtypeStruct((R, MEM_W), F32),
        jax.ShapeDtypeStruct((R, LRU_W), BF16),
        jax.ShapeDtypeStruct((CONV_LRU - 1, nb, LRU_W), F32),
        jax.ShapeDtypeStruct((nb, LRU_W), F32),
    ]
    return pl.pallas_call(
        _proj_sample_kernel,
        grid=(R // rb,),
        in_specs=[row(D)] + [_const_spec(c.shape) for c in consts] + [state, seq],
        out_specs=[row(FOX_W), row(FOX_W), row(FOX_W), row(N_HEADS), row(MEM_W), row(LRU_W),
                   state, seq],
        out_shape=out_shape,
        scratch_shapes=[pltpu.VMEM((LRU_W // LANES, rb, LANES), F32)] * 2,
        compiler_params=pltpu.CompilerParams(dimension_semantics=("arbitrary",),
                                             vmem_limit_bytes=VMEM_LIMIT),
        name="proj_sample",
    )(x2, *consts, cstate_tm, h0)


def _mem_sample_kernel(q_ref, mk_ref, mv_ref, o_ref):
    nb = mk_ref.shape[0]
    nt = q_ref.shape[0] // nb
    nrow = MEM_HEADS * nt
    row_head = lax.broadcasted_iota(jnp.int32, (nrow, mk_ref.shape[1]), 0) // nt
    col_head = lax.broadcasted_iota(jnp.int32, (nrow, mk_ref.shape[1]), 1) % MEM_HEADS
    own = row_head == col_head
    for b in range(nb):
        rows = slice(b * nt, (b + 1) * nt)
        q = jnp.concatenate([q_ref[rows, h * MEM_DIM:(h + 1) * MEM_DIM] for h in range(MEM_HEADS)],
                            axis=0).astype(BF16)
        s = jnp.where(own, _dot_nt(q, mk_ref[b].astype(BF16)), NEG_BIG)
        p = jnp.exp(s - jnp.max(s, axis=-1, keepdims=True))
        o = _dot(p.astype(BF16), mv_ref[b].astype(BF16)) / jnp.sum(p, axis=-1, keepdims=True)
        for h in range(MEM_HEADS):
            o_ref[rows, h * MEM_DIM:(h + 1) * MEM_DIM] = o[h * nt:(h + 1) * nt, :]


def _mem_sample(qm, mk, mv):
    R, W = qm.shape
    DB, MH, Dm = mk.shape
    nt = R // DB
    nb = MEM_BATCH
    kv_spec = pl.BlockSpec((nb, MH, Dm), lambda i: (i, 0, 0))
    return pl.pallas_call(
        _mem_sample_kernel,
        grid=(DB // nb,),
        in_specs=[pl.BlockSpec((nb * nt, W), lambda i: (i, 0)), kv_spec, kv_spec],
        out_specs=pl.BlockSpec((nb * nt, W), lambda i: (i, 0)),
        out_shape=jax.ShapeDtypeStruct((R, W), F32),
        compiler_params=pltpu.CompilerParams(dimension_semantics=("arbitrary",),
                                             vmem_limit_bytes=VMEM_LIMIT),
        name="mem_sample",
    )(qm, mk, mv)


def _fox_sample_kernel(pt_ref, q_ref, kn_ref, vn_ref, lfn_ref, k_hbm, v_hbm, f_hbm, o_ref,
                       k_buf, v_buf, f_buf, k_sem, v_sem, f_sem,
                       kb_scr, vb_scr, qbd_scr, m_scr, l_scr, acc_scr, c_scr, pack_scr):
    npg = PAGES_PER_STEP
    nslot = k_buf.shape[0]
    nt = q_ref.shape[0]
    nrow = N_HEADS * nt
    c = pl.program_id(1)
    nch = pl.num_programs(1)
    step = pl.program_id(0) * nch + c
    last = pl.num_programs(0) * nch - 1
    lane = lax.broadcasted_iota(jnp.int32, (nt, FOX_W), 1)
    streams = ((k_hbm, k_buf, k_sem), (v_hbm, v_buf, v_sem), (f_hbm, f_buf, f_sem))

    def page_copies(slot, pages):
        return [pltpu.make_async_copy(hbm.at[pages[p]], buf.at[slot, p], sem.at[slot])
                for hbm, buf, sem in streams for p in range(npg)]

    def fetch(s):
        src = jnp.minimum(s, last)
        row = src // nch
        col = (src % nch) * npg
        for cp in page_copies(s % nslot, [pt_ref[row, col + p] for p in range(npg)]):
            cp.start(priority=PAGE_DMA_PRIORITY)

    def wait(slot):
        for cp in page_copies(slot, [0] * npg):
            cp.wait()

    @pl.when(step == 0)
    def _():
        for s in range(nslot - 1):
            fetch(jnp.int32(s))

    slot = step % nslot
    wait(slot)
    fetch(step + (nslot - 1))
    k_pages = [k_buf.at[slot, p] for p in range(npg)]
    v_pages = [v_buf.at[slot, p] for p in range(npg)]
    f_pages = [f_buf.at[slot, p] for p in range(npg)]

    @pl.when(c == 0)
    def _():
        q = q_ref[...]
        for h in range(N_HEADS):
            mine = (lane >= h * HEAD_DIM) & (lane < (h + 1) * HEAD_DIM)
            qbd_scr[h * nt:(h + 1) * nt, :] = jnp.where(mine, q, 0.0).astype(BF16)
        m_scr[...] = jnp.full_like(m_scr, NEG_BIG)
        l_scr[...] = jnp.zeros_like(l_scr)
        acc_scr[...] = jnp.zeros_like(acc_scr)
        c_scr[...] = jnp.zeros_like(c_scr)

    def update(s, pv):
        m = m_scr[...]
        m_new = jnp.maximum(m, jnp.max(s, axis=-1, keepdims=True))
        alpha = jnp.exp(m - m_new)
        p = jnp.exp(s - m_new)
        l_scr[...] = alpha * l_scr[...] + jnp.sum(p, axis=-1, keepdims=True)
        acc_scr[...] = alpha * acc_scr[...] + pv(p.astype(BF16))
        m_scr[...] = m_new

    def head_rows(cb):
        return jnp.concatenate(
            [jnp.broadcast_to(cb[h:h + 1, :], (nt, cb.shape[1])) for h in range(N_HEADS)], axis=0)

    z = _lane_cumsum(jnp.concatenate([f_pages[p][...] for p in range(npg)], axis=0), PAGE)
    tiles = [z[p * N_HEADS:(p + 1) * N_HEADS, :] for p in range(npg)]
    totals = [jnp.broadcast_to(t[:, PAGE - 1:PAGE], t.shape) for t in tiles]
    off = c_scr[...]
    for p in range(npg):
        tiles[p] = tiles[p] + off
        off = off + totals[p]
    c_scr[...] = off

    for p in range(npg):
        kb_scr[:, p * PAGE:(p + 1) * PAGE] = k_pages[p][...].astype(BF16)
        vb_scr[:, p * PAGE:(p + 1) * PAGE] = v_pages[p][...].astype(BF16)
    s = _dot(qbd_scr[...], kb_scr[...]) - head_rows(jnp.concatenate(tiles, axis=1))
    update(s, lambda p: _dot_nt(p, vb_scr[...]))

    @pl.when(c == pl.num_programs(1) - 1)
    def _():
        zeros = jnp.zeros((PAGE - nt, FOX_W), F32)
        kn = jnp.concatenate([kn_ref[...], zeros], axis=0).astype(BF16)
        vn = jnp.concatenate([vn_ref[...], zeros], axis=0).astype(BF16)
        pack_scr[...] = jnp.zeros_like(pack_scr)
        pack_scr[0:nt, 0:N_HEADS] = lfn_ref[0]
        cn = _lane_cumsum(pack_scr[...].T[:N_HEADS, :], PAGE) + c_scr[...]
        s = _dot_nt(qbd_scr[...], kn) - head_rows(cn)
        tok = lax.broadcasted_iota(jnp.int32, (nrow, PAGE), 0) % nt
        key = lax.broadcasted_iota(jnp.int32, (nrow, PAGE), 1)
        update(jnp.where(key <= tok, s, NEG_BIG), lambda p: _dot(p, vn))
        acc = acc_scr[...] / l_scr[...]
        out = acc[0:nt, :]
        for h in range(1, N_HEADS):
            out = jnp.where(lane >= h * HEAD_DIM, acc[h * nt:(h + 1) * nt, :], out)
        o_ref[...] = out

    @pl.when(step == last)
    def _():
        for ahead in range(1, nslot):
            wait((step + ahead) % nslot)


N_FOX_SAMPLE_IN = 7
N_FOX_SAMPLE_SCRATCH = 14
N_PROJ_PROMPT_IN = 14
N_PROJ_PROMPT_OUT = 11


def _fox_sample_proj_prompt_kernel(steps_per_tile, tiles_per_seq, pt_ref, *refs):
    bounds = [N_FOX_SAMPLE_IN, N_PROJ_PROMPT_IN, 1, N_PROJ_PROMPT_OUT, N_FOX_SAMPLE_SCRATCH]
    parts, at = [], 0
    for n in bounds:
        parts.append(refs[at:at + n])
        at += n
    fox_in, proj_in, fox_out, proj_out, fox_scratch = parts
    proj_scratch = refs[at:]
    _fox_sample_kernel(pt_ref, *fox_in, *fox_out, *fox_scratch)
    step = pl.program_id(0) * pl.num_programs(1) + pl.program_id(1)

    @pl.when(step % steps_per_tile == 0)
    def _():
        _proj_prompt_tile((step // steps_per_tile) % tiles_per_seq,
                          *proj_in, *proj_out, *proj_scratch)


def _fox_sample_proj_prompt(page_table, q, k_new, v_new, logf_new, cache_kt, cache_vt, cache_ft,
                            x, mk, mv, w):
    DB, npages = page_table.shape
    R, W = q.shape
    nt = R // DB
    npg = PAGES_PER_STEP
    nk = npg * PAGE
    nch = npages // npg
    nslot = FETCH_SLOTS
    tiles_per_seq = x.shape[1] // TM_PROJ
    steps_per_tile = (DB * nch) // (x.shape[0] * tiles_per_seq)
    assert steps_per_tile * x.shape[0] * tiles_per_seq == DB * nch

    last_tile = x.shape[0] * tiles_per_seq - 1

    def tile_of(b, c, pt, ahead=0):
        tile = jnp.minimum((b * nch + c + ahead) // steps_per_tile, last_tile)
        return tile // tiles_per_seq, tile % tiles_per_seq

    proj_args, proj_in_specs, proj_out_specs, proj_out_shape, proj_scratch = (
        _proj_prompt_call_parts(x, mk, mv, w, functools.partial(tile_of, ahead=nslot - 1), tile_of))
    assert (len(proj_args), len(proj_out_specs)) == (N_PROJ_PROMPT_IN, N_PROJ_PROMPT_OUT)
    tok = pl.BlockSpec((nt, W), lambda b, c, pt: (b, 0))
    hbm = pl.BlockSpec(memory_space=pl.ANY)
    fox_in_specs = [tok, tok, tok, pl.BlockSpec((1, nt, N_HEADS), lambda b, c, pt: (b, 0, 0)),
                    hbm, hbm, hbm]
    fox_scratch = [pltpu.VMEM((nslot, npg, W, PAGE), F32),
                   pltpu.VMEM((nslot, npg, W, PAGE), F32),
                   pltpu.VMEM((nslot, npg, N_HEADS, PAGE), F32),
                   pltpu.SemaphoreType.DMA((nslot,)), pltpu.SemaphoreType.DMA((nslot,)),
                   pltpu.SemaphoreType.DMA((nslot,)),
                   pltpu.VMEM((W, nk), BF16), pltpu.VMEM((W, nk), BF16),
                   pltpu.VMEM((N_HEADS * nt, W), BF16),
                   pltpu.VMEM((N_HEADS * nt, 1), F32), pltpu.VMEM((N_HEADS * nt, 1), F32),
                   pltpu.VMEM((N_HEADS * nt, W), F32),
                   pltpu.VMEM((N_HEADS, PAGE), F32), pltpu.VMEM((PAGE, LANES), F32)]
    assert (len(fox_in_specs), len(fox_scratch)) == (N_FOX_SAMPLE_IN, N_FOX_SAMPLE_SCRATCH)
    grid_spec = pltpu.PrefetchScalarGridSpec(
        num_scalar_prefetch=1,
        grid=(DB, nch),
        in_specs=fox_in_specs + proj_in_specs,
        out_specs=[tok] + proj_out_specs,
        scratch_shapes=fox_scratch + proj_scratch,
    )
    return pl.pallas_call(
        functools.partial(_fox_sample_proj_prompt_kernel, steps_per_tile, tiles_per_seq),
        grid_spec=grid_spec,
        out_shape=[jax.ShapeDtypeStruct((R, W), F32)] + proj_out_shape,
        compiler_params=pltpu.CompilerParams(dimension_semantics=("arbitrary", "arbitrary"),
                                             vmem_limit_bytes=VMEM_LIMIT),
        name="fox_sample_proj_prompt",
    )(page_table, q, k_new, v_new, logf_new, cache_kt, cache_vt, cache_ft, *proj_args)


def _merge_ffn_sample_kernel(x_ref, ofox_ref, olru_ref, omem_ref, fstate_ref, gmix_ref, wgate_ref,
                             bgate_ref, wofox_ref, wolru_ref, womem_ref, wout_ref, gffn_ref,
                             wfg_ref, wfu_ref, wfc_ref, bfc_ref, wfd_ref, gfin_ref,
                             y_ref, gtail_ref, g_scr, gc_scr):
    nb = fstate_ref.shape[1]
    nt = x_ref.shape[0] // nb
    dff = wfg_ref.shape[1]
    h = _merge(x_ref[...], [(ofox_ref[...].astype(BF16), wofox_ref), (olru_ref[...], wolru_ref),
                            (omem_ref[...].astype(BF16), womem_ref)],
               gmix_ref, wgate_ref, bgate_ref, wout_ref)
    v2 = _rms(h, gffn_ref[...]).astype(BF16)
    y = h
    for c in range(dff // FF_CHUNK):
        sl = slice(c * FF_CHUNK, (c + 1) * FF_CHUNK)
        _park(g_scr, _dot(v2, wfg_ref[:, sl]))
        gs = [fstate_ref[j, :, sl] for j in range(CONV_FFN - 1)]
        gs += [_token_rows(g_scr, t, nb, nt) for t in range(nt)]
        for t in range(nt):
            gc = bfc_ref[:, sl]
            for j in range(CONV_FFN):
                gc = gc + wfc_ref[j:j + 1, sl] * gs[t + j]
            _set_token_rows(gc_scr, t, nb, nt, gc)
        for j in range(CONV_FFN - 1):
            gtail_ref[j, :, sl] = gs[nt + j]
        act = (_gelu(_unpark(gc_scr)) * _dot(v2, wfu_ref[:, sl])).astype(BF16)
        y = y + _dot(act, wfd_ref[sl, :])
    y_ref[...] = _rms(y, gfin_ref[...])


def _merge_ffn_sample(x2, ofox, olru, omem, fstate_tm, w):
    R, D = x2.shape
    nb = fstate_tm.shape[1]
    nt = R // nb
    dff = w["w_ffn_gate"].shape[1]
    rb = SAMPLE_ROWS
    sb = rb // nt
    consts = _merge_ffn_consts(w)
    row = lambda width: pl.BlockSpec((rb, width), lambda i: (i, 0))
    state = pl.BlockSpec((CONV_FFN - 1, sb, dff), lambda i: (0, i, 0))
    return pl.pallas_call(
        _merge_ffn_sample_kernel,
        grid=(R // rb,),
        in_specs=[row(D), row(FOX_W), row(LRU_W), row(MEM_W), state]
                 + [_const_spec(c.shape) for c in consts],
        out_specs=[row(D), state],
        out_shape=[jax.ShapeDtypeStruct((R, D), F32),
                   jax.ShapeDtypeStruct((CONV_FFN - 1, nb, dff), F32)],
        scratch_shapes=[pltpu.VMEM((FF_CHUNK // LANES, rb, LANES), F32)] * 2,
        compiler_params=pltpu.CompilerParams(dimension_semantics=("arbitrary",),
                                             vmem_limit_bytes=VMEM_LIMIT),
        name="merge_ffn_sample",
    )(x2, ofox, olru, omem, fstate_tm, *consts)


def _block_diag(wb):
    nb, bs, _ = wb.shape
    eye = jnp.eye(nb, dtype=wb.dtype)
    return (eye[:, None, :, None] * wb[:, :, None, :]).reshape(nb * bs, nb * bs)


def _prep_weights(l, g_mix, w_in, b_f, w_o_fox, w_lru_conv, b_lru_conv, w_lru_a, b_lru_a, w_lru_x,
                  b_lru_x, lru_lambda, w_o_lru, w_o_mem, w_gate, b_gate, w_out, g_ffn, w_ffn_gate,
                  w_ffn_up, w_ffn_conv, b_ffn_conv, w_ffn_down, g_final):
    wi = w_in[l]
    o_f = 3 * FOX_W
    o_r = o_f + N_HEADS
    o_m = o_r + 2 * LRU_W
    row = lambda v: v.reshape(1, -1)
    w_qkv = jnp.concatenate([wi[:, :FOX_W] * (HEAD_DIM ** -0.5), wi[:, FOX_W:o_f]], axis=1)
    return {
        "g_mix": row(g_mix[l]),
        "w_qkv": w_qkv.astype(BF16),
        "w_f": jnp.pad(wi[:, o_f:o_r], ((0, 0), (0, LANES - N_HEADS))).astype(BF16),
        "b_f": jnp.pad(row(b_f[l]), ((0, 0), (0, LANES - N_HEADS))),
        "w_rnn": wi[:, o_r:o_m].astype(BF16),
        "w_qm": wi[:, o_m:].astype(BF16),
        "w_lru_conv": w_lru_conv[l],
        "b_lru_conv": row(b_lru_conv[l]),
        "w_ax": jnp.concatenate([_block_diag(w_lru_a[l]), _block_diag(w_lru_x[l])],
                                axis=1).astype(BF16),
        "b_ax": jnp.concatenate([row(b_lru_a[l]), row(b_lru_x[l])], axis=1),
        "lam": row(lru_lambda[l]),
        "w_gate": w_gate[l].astype(BF16),
        "b_gate": row(b_gate[l]),
        "w_o_fox": w_o_fox[l].astype(BF16),
        "w_o_lru": w_o_lru[l].astype(BF16),
        "w_o_mem": w_o_mem[l].astype(BF16),
        "w_out": w_out[l].astype(BF16),
        "g_ffn": row(g_ffn[l]),
        "w_ffn_gate": w_ffn_gate[l].astype(BF16),
        "w_ffn_up": w_ffn_up[l].astype(BF16),
        "w_ffn_conv": w_ffn_conv[l],
        "b_ffn_conv": row(b_ffn_conv[l]),
        "w_ffn_down": w_ffn_down[l].astype(BF16),
        "g_final": row(g_final),
    }


def kernel(x_prompt, x_sample, mem_prompt, cache_k, cache_v, cache_logf, cache_mem_k, cache_mem_v,
           state_lru_h, state_lru_conv, state_ffn_conv, page_table,
           g_mix, w_in, b_f, w_o_fox, w_lru_conv, b_lru_conv, w_lru_a, b_lru_a, w_lru_x, b_lru_x,
           lru_lambda, w_o_lru, g_mem, w_mem_kv, w_o_mem, w_gate, b_gate, w_out,
           g_ffn, w_ffn_gate, w_ffn_up, w_ffn_conv, b_ffn_conv, w_ffn_down, g_final):
    depth = w_in.shape[0]
    assert depth == 1, "the final norm is fused into the single layer"
    l = 0
    B, S, D = x_prompt.shape
    DB, T, _ = x_sample.shape
    n_pool = cache_k.shape[1]
    w = _prep_weights(l, g_mix, w_in, b_f, w_o_fox, w_lru_conv, b_lru_conv, w_lru_a, b_lru_a,
                      w_lru_x, b_lru_x, lru_lambda, w_o_lru, w_o_mem, w_gate, b_gate, w_out,
                      g_ffn, w_ffn_gate, w_ffn_up, w_ffn_conv, b_ffn_conv, w_ffn_down, g_final)

    mk_p, mv_p = _mem_kv(mem_prompt, g_mem[l].reshape(1, -1), w_mem_kv[l].astype(BF16))

    x2 = x_sample.reshape(DB * T, D)
    cstate_tm = jnp.swapaxes(state_lru_conv[l], 0, 1)
    fstate_tm = jnp.swapaxes(state_ffn_conv[l], 0, 1)
    (q_s, k_s, v_s, logf_s, qm_s, olru_s, xtail_s, hlast_s) = _proj_sample(
        x2, cstate_tm, state_lru_h[l], w)
    omem_s = _mem_sample(qm_s, cache_mem_k[l].reshape(DB, -1, MEM_DIM),
                         cache_mem_v[l].reshape(DB, -1, MEM_DIM))
    cache_kt = jnp.transpose(cache_k[l], (0, 2, 3, 1)).reshape(n_pool, FOX_W, PAGE)
    cache_vt = jnp.transpose(cache_v[l], (0, 2, 3, 1)).reshape(n_pool, FOX_W, PAGE)
    cache_ft = jnp.transpose(cache_logf[l], (0, 2, 1))
    (ofox_s, k_p, v_p, logf_p, ck_p, qb, kb, vb, olru_p, omem_p, xtail_p, htail_p) = (
        _fox_sample_proj_prompt(page_table, q_s, k_s, v_s, logf_s.reshape(DB, T, N_HEADS),
                                cache_kt, cache_vt, cache_ft, x_prompt, mk_p, mv_p, w))
    y_s, gtail_s = _merge_ffn_sample(x2, ofox_s, olru_s, omem_s, fstate_tm, w)

    ofox_p = _fox_prompt(qb, kb, vb, ck_p)
    y_p, gtail_p = _merge_ffn_prompt(x_prompt, ofox_p, olru_p, omem_p, w)

    heads = lambda a, n: a.reshape(1, n, -1, N_HEADS, HEAD_DIM)
    mem_heads = lambda a: a.reshape(1, B, -1, MEM_HEADS, MEM_DIM)
    return (
        y_p,
        y_s.reshape(DB, T, D),
        heads(k_p, B), heads(v_p, B), logf_p[None],
        mem_heads(mk_p), mem_heads(mv_p),
        htail_p[None, :, -1, :],
        xtail_p[None, :, SUBLANES - (CONV_LRU - 1):, :],
        gtail_p[None, :, SUBLANES - (CONV_FFN - 1):, :],
        heads(k_s, DB), heads(v_s, DB), logf_s.reshape(1, DB, T, N_HEADS),
        hlast_s[None],
        jnp.swapaxes(xtail_s, 0, 1)[None],
        jnp.swapaxes(gtail_s, 0, 1)[None],
    )
```
